```python
import jax, jax.numpy as jnp
from jax import lax
import numpy as np

D_MODEL = 1024
BATCH = 16
SEQ = 2048
DEPTH = 2
DEC_BATCH = 2
DEC_SEQ = 8192
PAST_LEN = 128

N_MIXERS = 2
N_POOL_LAYERS = (DEPTH + 1) // 2
N_ATTN_LAYERS = DEPTH // 2
EPS = 1e-6
POOL_WINDOWS = (2, 4, 8, 16)
N_GROUPS = len(POOL_WINDOWS)
GROUP = D_MODEL // N_GROUPS
HEAD_DIM = 64
N_HEADS = D_MODEL // HEAD_DIM
N_KV = 4
GQA_GROUP = N_HEADS // N_KV
WINDOW = 128
BLOCK = 128
KEYS = BLOCK + 2 * WINDOW
ROPE_THETA = 10000.0
N_MEM = 256
X_HEADS = 4
X_HEAD_DIM = D_MODEL // X_HEADS
D_FF = 4 * D_MODEL

kernel_name = "hybrid_pool_window_gqa_memory_encoder"


def rmsnorm(x, g):
    xf = x.astype(jnp.float32)
    y = xf * lax.rsqrt(jnp.mean(xf * xf, axis=-1, keepdims=True) + EPS)
    return (y * g.astype(jnp.float32)).astype(x.dtype)


def rope_tables(seq_len):
    inv_freq = ROPE_THETA ** (-jnp.arange(0, HEAD_DIM, 2, dtype=jnp.float32) / HEAD_DIM)
    ang = jnp.arange(seq_len, dtype=jnp.float32)[:, None] * inv_freq[None, :]
    return jnp.cos(ang), jnp.sin(ang)


def apply_rope(x, cos, sin):
    xf = x.astype(jnp.float32)
    x1, x2 = jnp.split(xf, 2, axis=-1)
    c = cos[None, :, None, :]
    s = sin[None, :, None, :]
    return jnp.concatenate([x1 * c - x2 * s, x2 * c + x1 * s], axis=-1).astype(x.dtype)


def pool_mixer(h, w_pool, scale):
    B, S, D = h.shape
    hf = h.astype(jnp.float32)
    cs = jnp.concatenate([jnp.zeros((B, 1, D), jnp.float32), jnp.cumsum(hf, axis=1)], axis=1)
    t = jnp.arange(S)
    pooled = []
    for g, w in enumerate(POOL_WINDOWS):
        lo = jnp.clip(t - w // 2, 0, S)
        hi = jnp.clip(t + w // 2, 0, S)
        count = (hi - lo).astype(jnp.float32)
        seg = cs[:, :, g * GROUP:(g + 1) * GROUP]
        total = jnp.take(seg, hi, axis=1) - jnp.take(seg, lo, axis=1)
        pooled.append(total / count[None, :, None])
    diff = jnp.concatenate(pooled, axis=-1) - hf
    y = jnp.einsum("bsgc,gcd->bsgd", diff.reshape(B, S, N_GROUPS, GROUP), w_pool.astype(jnp.float32))
    y = y.reshape(B, S, D) * scale.astype(jnp.float32)
    return y.astype(h.dtype)


def window_attention(h, w_qkv, w_o, sinks, cos, sin):
    B, S, _ = h.shape
    nb = S // BLOCK
    qkv = h @ w_qkv
    q = qkv[..., :N_HEADS * HEAD_DIM].reshape(B, S, N_HEADS, HEAD_DIM)
    k = qkv[..., N_HEADS * HEAD_DIM:(N_HEADS + N_KV) * HEAD_DIM].reshape(B, S, N_KV, HEAD_DIM)
    v = qkv[..., (N_HEADS + N_KV) * HEAD_DIM:].reshape(B, S, N_KV, HEAD_DIM)
    q = apply_rope(q, cos, sin) * (HEAD_DIM ** -0.5)
    k = apply_rope(k, cos, sin)
    pad = ((0, 0), (WINDOW, WINDOW), (0, 0), (0, 0))
    kp = jnp.pad(k, pad)
    vp = jnp.pad(v, pad)
    qb = q.reshape(B, nb, BLOCK, N_KV, GQA_GROUP, HEAD_DIM).transpose(1, 0, 2, 3, 4, 5)
    r = jnp.arange(BLOCK)
    c = jnp.arange(KEYS)
    sink = sinks.astype(jnp.float32).reshape(N_KV, GQA_GROUP)

    def block(args):
        i, qi = args
        ki = lax.dynamic_slice_in_dim(kp, i * BLOCK, KEYS, axis=1)
        vi = lax.dynamic_slice_in_dim(vp, i * BLOCK, KEYS, axis=1)
        s = jnp.einsum("bqkgd,bskd->bkgqs", qi, ki).astype(jnp.float32)
        qpos = i * BLOCK + r
        kpos = i * BLOCK - WINDOW + c
        mask = (jnp.abs(kpos[None, :] - qpos[:, None]) <= WINDOW) & (kpos[None, :] >= 0) & (kpos[None, :] < S)
        s = jnp.where(mask[None, None, None], s, -1e30)
        sink_col = jnp.broadcast_to(sink[None, :, :, None, None], s.shape[:-1] + (1,))
        p = jax.nn.softmax(jnp.concatenate([s, sink_col], axis=-1), axis=-1)[..., :-1]
        o = jnp.einsum("bkgqs,bskd->bqkgd", p.astype(vi.dtype), vi)
        return o.reshape(B, BLOCK, N_HEADS * HEAD_DIM)

    o = lax.map(block, (jnp.arange(nb), qb))
    o = o.transpose(1, 0, 2, 3).reshape(B, S, N_HEADS * HEAD_DIM)
    return o @ w_o


def memory_attention(h, mem_n, w_q, w_kv, w_o):
    B, S, _ = h.shape
    q = (h @ w_q).reshape(B, S, X_HEADS, X_HEAD_DIM) * (X_HEAD_DIM ** -0.5)
    kv = mem_n @ w_kv
    k = kv[..., :D_MODEL].reshape(B, N_MEM, X_HEADS, X_HEAD_DIM)
    v = kv[..., D_MODEL:].reshape(B, N_MEM, X_HEADS, X_HEAD_DIM)
    s = jnp.einsum("bshd,bmhd->bhsm", q, k).astype(jnp.float32)
    p = jax.nn.softmax(s, axis=-1).astype(v.dtype)
    o = jnp.einsum("bhsm,bmhd->bshd", p, v).reshape(B, S, D_MODEL)
    return o @ w_o


def sq_relu_mlp(h, w_up, w_down):
    u = jax.nn.relu(h @ w_up)
    return (u * u) @ w_down


def trunk(x, mem, norm_mix, pool_w, pool_scale, attn_qkv, attn_o, attn_sink,
          norm_x, norm_mem, x_wq, x_wkv, x_wo, norm_mlp, w_up, w_down, norm_final):
    S = x.shape[1]
    cos, sin = rope_tables(S)
    for i in range(DEPTH):
        j = i // N_MIXERS
        hn = rmsnorm(x, norm_mix[i])
        if i % N_MIXERS == 0:
            x = x + pool_mixer(hn, pool_w[j], pool_scale[j])
        else:
            x = x + window_attention(hn, attn_qkv[j], attn_o[j], attn_sink[j], cos, sin)
        x = x + memory_attention(rmsnorm(x, norm_x[i]), rmsnorm(mem, norm_mem[i]), x_wq[i], x_wkv[i], x_wo[i])
        x = x + sq_relu_mlp(rmsnorm(x, norm_mlp[i]), w_up[i], w_down[i])
    return rmsnorm(x, norm_final)


def setup_inputs(seed: int = 0) -> dict:
    key = jax.random.key(seed)
    ks = jax.random.split(key, 24)
    f32 = jnp.float32

    def nrm(k, shape, scale):
        return jax.random.normal(k, shape, f32) * scale

    def gain(k, shape):
        return 1.0 + 0.05 * jax.random.normal(k, shape, f32)

    D = D_MODEL
    return {
        "x_prompt": nrm(ks[0], (BATCH, SEQ, D), 1.0),
        "x_sample": nrm(ks[1], (DEC_BATCH, DEC_SEQ, D), 1.0),
        "mem_prompt": nrm(ks[2], (BATCH, N_MEM, D), 1.0),
        "mem_sample": nrm(ks[3], (DEC_BATCH, N_MEM, D), 1.0),
        "norm_mix": gain(ks[4], (DEPTH, D)),
        "pool_w": nrm(ks[5], (N_POOL_LAYERS, N_GROUPS, GROUP, GROUP), GROUP ** -0.5),
        "pool_scale": 0.5 + 0.1 * jax.random.normal(ks[6], (N_POOL_LAYERS, D), f32),
        "attn_qkv": nrm(ks[7], (N_ATTN_LAYERS, D, (N_HEADS + 2 * N_KV) * HEAD_DIM), D ** -0.5),
        "attn_o": nrm(ks[8], (N_ATTN_LAYERS, N_HEADS * HEAD_DIM, D), (N_HEADS * HEAD_DIM) ** -0.5),
        "attn_sink": nrm(ks[9], (N_ATTN_LAYERS, N_HEADS), 0.5),
        "norm_x": gain(ks[10], (DEPTH, D)),
        "norm_mem": gain(ks[11], (DEPTH, D)),
        "x_wq": nrm(ks[12], (DEPTH, D, D), D ** -0.5),
        "x_wkv": nrm(ks[13], (DEPTH, D, 2 * D), D ** -0.5),
        "x_wo": nrm(ks[14], (DEPTH, D, D), D ** -0.5),
        "norm_mlp": gain(ks[15], (DEPTH, D)),
        "w_up": nrm(ks[16], (DEPTH, D, D_FF), D ** -0.5),
        "w_down": nrm(ks[17], (DEPTH, D_FF, D), D_FF ** -0.5),
        "norm_final": gain(ks[18], (D,)),
    }


def reference(x_prompt, x_sample, mem_prompt, mem_sample, norm_mix, pool_w, pool_scale,
              attn_qkv, attn_o, attn_sink, norm_x, norm_mem, x_wq, x_wkv, x_wo,
              norm_mlp, w_up, w_down, norm_final):
    y_prompt = trunk(x_prompt, mem_prompt, norm_mix, pool_w, pool_scale, attn_qkv, attn_o, attn_sink,
                     norm_x, norm_mem, x_wq, x_wkv, x_wo, norm_mlp, w_up, w_down, norm_final)
    y_sample = trunk(x_sample, mem_sample, norm_mix, pool_w, pool_scale, attn_qkv, attn_o, attn_sink,
                     norm_x, norm_mem, x_wq, x_wkv, x_wo, norm_mlp, w_up, w_down, norm_final)
    return (y_prompt, y_sample)
```

```python
import functools

import jax
import jax.numpy as jnp
from jax import lax
from jax.experimental import pallas as pl
from jax.experimental.pallas import tpu as pltpu

D_MODEL = 1024
DEPTH = 2
EPS = 1e-6
POOL_WINDOWS = (2, 4, 8, 16)
GROUP = D_MODEL // len(POOL_WINDOWS)
HEAD_DIM = 64
N_HEADS = 16
N_KV = 4
GQA_GROUP = N_HEADS // N_KV
WINDOW = 128
ROPE_THETA = 10000.0
N_MEM = 256
X_HEADS = 4
X_HEAD_DIM = D_MODEL // X_HEADS
D_FF = 4 * D_MODEL

LANES = 128
POOL_HALO = 16
VMEM_LIMIT = 56 * 1024 * 1024

TOKEN_TILE = 512
FF_CHUNK = 1024
Q_BLOCK = 128

BF16 = jnp.bfloat16
F32 = jnp.float32


def _params():
    return pltpu.CompilerParams(vmem_limit_bytes=VMEM_LIMIT)


def _resident(shape):
    zeros = (0,) * len(shape)
    return pl.BlockSpec(shape, lambda *_: zeros, pipeline_mode=pl.Buffered(1))


def _rms(x, g):
    ms = jnp.mean(x * x, axis=-1, keepdims=True)
    return (x * lax.rsqrt(ms + EPS)) * g


def _dot(a, b):
    return jnp.dot(a, b, preferred_element_type=F32)


def _dot_nt(a, b):
    return lax.dot_general(a, b, (((1,), (1,)), ((), ())), preferred_element_type=F32)


def _pool_kernel(xc_ref, xp_ref, xn_ref, g_ref, w_ref, sc_ref, o_ref, hbuf, *, seq_len, tile):
    i = pl.program_id(1)
    last = pl.num_programs(1) - 1
    g = g_ref[...]
    x = xc_ref[0]
    hn = _rms(x, g)
    hp = _rms(xp_ref[0], g) * jnp.where(i > 0, 1.0, 0.0)
    hx = _rms(xn_ref[0], g) * jnp.where(i < last, 1.0, 0.0)
    hbuf[0:POOL_HALO, :] = hp
    hbuf[POOL_HALO:POOL_HALO + tile, :] = hn
    hbuf[POOL_HALO + tile:, :] = hx

    t = i * tile + lax.broadcasted_iota(jnp.int32, (tile, 1), 0)
    for gi, w in enumerate(POOL_WINDOWS):
        cs = slice(gi * GROUP, (gi + 1) * GROUP)
        total = hbuf[pl.ds(POOL_HALO - w // 2, tile), cs]
        for j in range(-w // 2 + 1, w // 2):
            total = total + hbuf[pl.ds(POOL_HALO + j, tile), cs]
        lo = jnp.maximum(t - w // 2, 0)
        hi = jnp.minimum(t + w // 2, seq_len)
        count = (hi - lo).astype(F32)
        diff = total / count - hn[:, cs]
        y = _dot(diff.astype(BF16), w_ref[gi])
        o_ref[0, :, cs] = x[:, cs] + y * sc_ref[:, cs]


def _pool_layer(x, g, w, scale):
    B, S, D = x.shape
    tile = TOKEN_TILE
    nt = S // tile
    hb = tile // POOL_HALO
    nhb = S // POOL_HALO
    kern = functools.partial(_pool_kernel, seq_len=S, tile=tile)
    return pl.pallas_call(
        kern,
        grid=(B, nt),
        in_specs=[
            pl.BlockSpec((1, tile, D), lambda b, i: (b, i, 0)),
            pl.BlockSpec((1, POOL_HALO, D), lambda b, i: (b, jnp.maximum(i * hb - 1, 0), 0)),
            pl.BlockSpec((1, POOL_HALO, D), lambda b, i: (b, jnp.minimum((i + 1) * hb, nhb - 1), 0)),
            _resident((1, D)),
            _resident(w.shape),
            _resident((1, D)),
        ],
        out_specs=pl.BlockSpec((1, tile, D), lambda b, i: (b, i, 0)),
        out_shape=jax.ShapeDtypeStruct(x.shape, F32),
        scratch_shapes=[pltpu.VMEM((tile + 2 * POOL_HALO, D), F32)],
        compiler_params=_params(),
        name="pool_mixer",
    )(x, x, x, g, w, scale)


def _memkv_kernel(m_ref, g_ref, w_ref, k_ref, v_ref):
    mn = _rms(m_ref[0], g_ref[0]).astype(BF16)
    kv = _dot(mn, w_ref[0])
    k_ref[0, 0] = kv[:, :D_MODEL].astype(BF16)
    v_ref[0, 0] = kv[:, D_MODEL:].astype(BF16)


def _mem_kv(mem, g, w_kv):
    B = mem.shape[0]
    out = jax.ShapeDtypeStruct((DEPTH, B, N_MEM, D_MODEL), BF16)
    return pl.pallas_call(
        _memkv_kernel,
        grid=(DEPTH, B),
        in_specs=[
            pl.BlockSpec((1, N_MEM, D_MODEL), lambda l, b: (b, 0, 0)),
            pl.BlockSpec((1, 1, D_MODEL), lambda l, b: (l, 0, 0)),
            pl.BlockSpec((1, D_MODEL, 2 * D_MODEL), lambda l, b: (l, 0, 0)),
        ],
        out_specs=[
            pl.BlockSpec((1, 1, N_MEM, D_MODEL), lambda l, b: (l, b, 0, 0)),
            pl.BlockSpec((1, 1, N_MEM, D_MODEL), lambda l, b: (l, b, 0, 0)),
        ],
        out_shape=[out, out],
        compiler_params=_params(),
        name="mem_kv",
    )(mem, g, w_kv)


def _xattn_kernel(x_ref, g_ref, wq_ref, k_ref, v_ref, wo_ref, o_ref):
    x = x_ref[...]
    hn = _rms(x, g_ref[...]).astype(BF16)
    q = (_dot(hn, wq_ref[...]) * (X_HEAD_DIM ** -0.5)).astype(BF16)
    heads = []
    for h in range(X_HEADS):
        cs = slice(h * X_HEAD_DIM, (h + 1) * X_HEAD_DIM)
        s = _dot_nt(q[:, cs], k_ref[0, 0, :, cs])
        m = jnp.max(s, axis=-1, keepdims=True)
        e = jnp.exp(s - m)
        p = e / jnp.sum(e, axis=-1, keepdims=True)
        heads.append(_dot(p.astype(BF16), v_ref[0, 0, :, cs]).astype(BF16))
    o = jnp.concatenate(heads, axis=-1)
    o_ref[...] = x + _dot(o, wo_ref[...])


def _xattn_layer(x2d, seq_len, g, wq, k, v, wo, layer):
    N, D = x2d.shape
    tile = TOKEN_TILE
    tps = seq_len // tile
    kv_spec = pl.BlockSpec((1, 1, N_MEM, D), lambda i: (layer, i // tps, 0, 0))
    return pl.pallas_call(
        _xattn_kernel,
        grid=(N // tile,),
        in_specs=[
            pl.BlockSpec((tile, D), lambda i: (i, 0)),
            _resident((1, D)),
            _resident((D, D)),
            kv_spec,
            kv_spec,
            _resident((D, D)),
        ],
        out_specs=pl.BlockSpec((tile, D), lambda i: (i, 0)),
        out_shape=jax.ShapeDtypeStruct((N, D), F32),
        compiler_params=_params(),
        name="mem_xattn",
    )(x2d, g, wq, k, v, wo)


def _mlp_kernel(x_ref, g_ref, wu_ref, wd_ref, gf_ref, o_ref, *, final_norm):
    x = x_ref[...]
    hn = _rms(x, g_ref[...]).astype(BF16)
    acc = x
    for c in range(D_FF // FF_CHUNK):
        cs = slice(c * FF_CHUNK, (c + 1) * FF_CHUNK)
        u = jnp.maximum(_dot(hn, wu_ref[:, cs]), 0.0)
        acc = acc + _dot((u * u).astype(BF16), wd_ref[cs, :])
    if final_norm:
        acc = _rms(acc, gf_ref[...])
    o_ref[...] = acc


def _mlp_layer(x2d, g, w_up, w_down, g_final, final_norm):
    N, D = x2d.shape
    tile = TOKEN_TILE
    kern = functools.partial(_mlp_kernel, final_norm=final_norm)
    return pl.pallas_call(
        kern,
        grid=(N // tile,),
        in_specs=[
            pl.BlockSpec((tile, D), lambda i: (i, 0)),
            _resident((1, D)),
            _resident((D, D_FF)),
            _resident((D_FF, D)),
            _resident((1, D)),
        ],
        out_specs=pl.BlockSpec((tile, D), lambda i: (i, 0)),
        out_shape=jax.ShapeDtypeStruct((N, D), F32),
        compiler_params=_params(),
        name="sq_relu_mlp",
    )(x2d, g, w_up, w_down, g_final)


def _rope_tables(seq_len):
    inv_freq = ROPE_THETA ** (-jnp.arange(0, HEAD_DIM, 2, dtype=F32) / HEAD_DIM)
    ang = jnp.arange(seq_len, dtype=F32)[:, None] * inv_freq[None, :]
    cos, sin = jnp.cos(ang), jnp.sin(ang)
    zero = jnp.zeros_like(sin)
    reps = LANES // HEAD_DIM
    cos_t = jnp.tile(jnp.concatenate([cos, cos], axis=-1), (1, reps))
    sa_t = jnp.tile(jnp.concatenate([-sin, zero], axis=-1), (1, reps))
    sb_t = jnp.tile(jnp.concatenate([zero, sin], axis=-1), (1, reps))
    return cos_t, sa_t, sb_t


def _qkv_kernel(x_ref, g_ref, w_ref, cos_ref, sa_ref, sb_ref, q_ref, k_ref, v_ref):
    hn = _rms(x_ref[...], g_ref[...]).astype(BF16)
    qkv = _dot(hn, w_ref[...])
    cos, sa, sb = cos_ref[...], sa_ref[...], sb_ref[...]
    half = HEAD_DIM // 2

    def rope(xb):
        return xb * cos + pltpu.roll(xb, LANES - half, 1) * sa + pltpu.roll(xb, half, 1) * sb

    nq = N_HEADS * HEAD_DIM
    nk = N_KV * HEAD_DIM
    for cb in range(nq // LANES):
        cs = slice(cb * LANES, (cb + 1) * LANES)
        q_ref[:, cs] = (rope(qkv[:, cs]) * (HEAD_DIM ** -0.5)).astype(BF16)
    for cb in range(nk // LANES):
        cs = slice(cb * LANES, (cb + 1) * LANES)
        k_ref[:, cs] = rope(qkv[:, nq + cb * LANES: nq + (cb + 1) * LANES]).astype(BF16)
    v_ref[...] = qkv[:, nq + nk:].astype(BF16)


def _qkv_layer(x2d, seq_len, g, w_qkv, tables):
    N, D = x2d.shape
    tile = TOKEN_TILE
    tps = seq_len // tile
    nq, nk = N_HEADS * HEAD_DIM, N_KV * HEAD_DIM
    tab_spec = pl.BlockSpec((tile, LANES), lambda i: (i % tps, 0))
    return pl.pallas_call(
        _qkv_kernel,
        grid=(N // tile,),
        in_specs=[
            pl.BlockSpec((tile, D), lambda i: (i, 0)),
            _resident((1, D)),
            _resident(w_qkv.shape),
            tab_spec, tab_spec, tab_spec,
        ],
        out_specs=[
            pl.BlockSpec((tile, nq), lambda i: (i, 0)),
            pl.BlockSpec((tile, nk), lambda i: (i, 0)),
            pl.BlockSpec((tile, nk), lambda i: (i, 0)),
        ],
        out_shape=[
            jax.ShapeDtypeStruct((N, nq), BF16),
            jax.ShapeDtypeStruct((N, nk), BF16),
            jax.ShapeDtypeStruct((N, nk), BF16),
        ],
        compiler_params=_params(),
        name="qkv_rope",
    )(x2d, g, w_qkv, *tables)


def _winattn_kernel(sink_ref, q_ref, kp_ref, kc_ref, kn_ref, vp_ref, vc_ref, vn_ref, o_ref):
    i = pl.program_id(1)
    last = pl.num_programs(1) - 1
    q = q_ref[0]
    k = jnp.concatenate([kp_ref[0], kc_ref[0], kn_ref[0]], axis=0)
    v = jnp.concatenate([vp_ref[0], vc_ref[0], vn_ref[0]], axis=0)
    nkeys = Q_BLOCK + 2 * WINDOW
    r = lax.broadcasted_iota(jnp.int32, (Q_BLOCK, nkeys), 0)
    c = lax.broadcasted_iota(jnp.int32, (Q_BLOCK, nkeys), 1)
    rel = c - WINDOW - r
    mask = (jnp.abs(rel) <= WINDOW)
    mask = mask & ((c >= WINDOW) | (i > 0)) & ((c < WINDOW + Q_BLOCK) | (i < last))
    for h in range(N_HEADS):
        kh = h // GQA_GROUP
        qs = slice(h * HEAD_DIM, (h + 1) * HEAD_DIM)
        ks = slice(kh * HEAD_DIM, (kh + 1) * HEAD_DIM)
        s = _dot_nt(q[:, qs], k[:, ks])
        s = jnp.where(mask, s, -1e30)
        sink = sink_ref[h]
        m = jnp.maximum(jnp.max(s, axis=-1, keepdims=True), sink)
        e = jnp.exp(s - m)
        denom = jnp.sum(e, axis=-1, keepdims=True) + jnp.exp(sink - m)
        p = (e / denom).astype(BF16)
        o_ref[0, :, qs] = _dot(p, v[:, ks]).astype(BF16)


def _winattn_layer(q, k, v, sinks):
    B, S, nq = q.shape
    nk = k.shape[-1]
    nb = S // Q_BLOCK
    q_spec = pl.BlockSpec((1, Q_BLOCK, nq), lambda b, i: (b, i, 0))
    prev_spec = pl.BlockSpec((1, Q_BLOCK, nk), lambda b, i: (b, jnp.maximum(i - 1, 0), 0))
    cur_spec = pl.BlockSpec((1, Q_BLOCK, nk), lambda b, i: (b, i, 0))
    next_spec = pl.BlockSpec((1, Q_BLOCK, nk), lambda b, i: (b, jnp.minimum(i + 1, nb - 1), 0))
    return pl.pallas_call(
        _winattn_kernel,
        grid=(B, nb),
        in_specs=[
            pl.BlockSpec(memory_space=pltpu.SMEM),
            q_spec,
            prev_spec, cur_spec, next_spec,
            prev_spec, cur_spec, next_spec,
        ],
        out_specs=q_spec,
        out_shape=jax.ShapeDtypeStruct((B, S, nq), BF16),
        compiler_params=_params(),
        name="window_attn",
    )(sinks, q, k, k, k, v, v, v)


def _oproj_kernel(x_ref, a_ref, w_ref, o_ref):
    o_ref[...] = x_ref[...] + _dot(a_ref[...], w_ref[...])


def _oproj_layer(x2d, a2d, w):
    N, D = x2d.shape
    tile = TOKEN_TILE
    return pl.pallas_call(
        _oproj_kernel,
        grid=(N // tile,),
        in_specs=[
            pl.BlockSpec((tile, D), lambda i: (i, 0)),
            pl.BlockSpec((tile, a2d.shape[1]), lambda i: (i, 0)),
            _resident(w.shape),
        ],
        out_specs=pl.BlockSpec((tile, D), lambda i: (i, 0)),
        out_shape=jax.ShapeDtypeStruct((N, D), F32),
        compiler_params=_params(),
        name="attn_oproj",
    )(x2d, a2d, w)


def _trunk(x, mem, p):
    B, S, D = x.shape
    N = B * S
    k_mem, v_mem = _mem_kv(mem, p["norm_mem"], p["x_wkv"])
    tables = _rope_tables(S)

    def row(a, i):
        return a[i].reshape(1, -1)

    x = _pool_layer(x, row(p["norm_mix"], 0), p["pool_w"][0], row(p["pool_scale"], 0))
    x2 = x.reshape(N, D)
    for i in range(DEPTH):
        if i % 2 == 1:
            j = i // 2
            q, k, v = _qkv_layer(x2, S, row(p["norm_mix"], i), p["attn_qkv"][j], tables)
            a = _winattn_layer(q.reshape(B, S, -1), k.reshape(B, S, -1), v.reshape(B, S, -1),
                               p["attn_sink"][j])
            x2 = _oproj_layer(x2, a.reshape(N, -1), p["attn_o"][j])
        x2 = _xattn_layer(x2, S, row(p["norm_x"], i), p["x_wq"][i], k_mem, v_mem, p["x_wo"][i], i)
        x2 = _mlp_layer(x2, row(p["norm_mlp"], i), p["w_up"][i], p["w_down"][i],
                        p["norm_final"].reshape(1, -1), final_norm=(i == DEPTH - 1))
    return x2.reshape(B, S, D)


def kernel(x_prompt, x_sample, mem_prompt, mem_sample, norm_mix, pool_w, pool_scale, attn_qkv, attn_o,
           attn_sink, norm_x, norm_mem, x_wq, x_wkv, x_wo, norm_mlp, w_up, w_down, norm_final):
    p = dict(
        norm_mix=norm_mix, pool_scale=pool_scale, attn_sink=attn_sink, norm_x=norm_x,
        norm_mem=norm_mem.reshape(DEPTH, 1, D_MODEL), norm_mlp=norm_mlp, norm_final=norm_final,
        pool_w=pool_w.astype(BF16), attn_qkv=attn_qkv.astype(BF16), attn_o=attn_o.astype(BF16),
        x_wq=x_wq.astype(BF16), x_wkv=x_wkv.astype(BF16), x_wo=x_wo.astype(BF16),
        w_up=w_up.astype(BF16), w_down=w_down.astype(BF16),
    )
    return (_trunk(x_prompt, mem_prompt, p), _trunk(x_sample, mem_sample, p))
```

```python
import functools

import jax
import jax.numpy as jnp
from jax import lax
from jax.experimental import pallas as pl
from jax.experimental.pallas import tpu as pltpu

D_MODEL = 1024
DEPTH = 2
EPS = 1e-6
POOL_WINDOWS = (2, 4, 8, 16)
GROUP = D_MODEL // len(POOL_WINDOWS)
HEAD_DIM = 64
N_HEADS = 16
N_KV = 4
GQA_GROUP = N_HEADS // N_KV
WINDOW = 128
ROPE_THETA = 10000.0
N_MEM = 256
X_HEADS = 4
X_HEAD_DIM = D_MODEL // X_HEADS
D_FF = 4 * D_MODEL

LANES = 128
POOL_HALO = 16
VMEM_LIMIT = 56 * 1024 * 1024

TOKEN_TILE = 512
FF_CHUNK = 1024
Q_BLOCK = 128
N_KEYS = Q_BLOCK + 2 * WINDOW
KV_WIDTH = N_KV * HEAD_DIM

BF16 = jnp.bfloat16
F32 = jnp.float32


def _params():
    return pltpu.CompilerParams(vmem_limit_bytes=VMEM_LIMIT)


def _resident(shape):
    zeros = (0,) * len(shape)
    return pl.BlockSpec(shape, lambda *_: zeros, pipeline_mode=pl.Buffered(1))


def _rms(x, g):
    ms = jnp.mean(x * x, axis=-1, keepdims=True)
    return (x * lax.rsqrt(ms + EPS)) * g


def _dot(a, b):
    return jnp.dot(a, b, preferred_element_type=F32)


def _dot_nt(a, b):
    return lax.dot_general(a, b, (((1,), (1,)), ((), ())), preferred_element_type=F32)


def _pool_kernel(xc_ref, xp_ref, xn_ref, g_ref, w_ref, sc_ref, o_ref, hbuf, *, seq_len, tile):
    i = pl.program_id(1)
    last = pl.num_programs(1) - 1
    g = g_ref[...]
    x = xc_ref[0]
    hn = _rms(x, g)
    hp = _rms(xp_ref[0], g) * jnp.where(i > 0, 1.0, 0.0)
    hx = _rms(xn_ref[0], g) * jnp.where(i < last, 1.0, 0.0)
    hbuf[0:POOL_HALO, :] = hp
    hbuf[POOL_HALO:POOL_HALO + tile, :] = hn
    hbuf[POOL_HALO + tile:, :] = hx

    t = i * tile + lax.broadcasted_iota(jnp.int32, (tile, 1), 0)
    for gi, w in enumerate(POOL_WINDOWS):
        cs = slice(gi * GROUP, (gi + 1) * GROUP)
        total = hbuf[pl.ds(POOL_HALO - w // 2, tile), cs]
        for j in range(-w // 2 + 1, w // 2):
            total = total + hbuf[pl.ds(POOL_HALO + j, tile), cs]
        lo = jnp.maximum(t - w // 2, 0)
        hi = jnp.minimum(t + w // 2, seq_len)
        count = (hi - lo).astype(F32)
        diff = total / count - hn[:, cs]
        y = _dot(diff.astype(BF16), w_ref[gi])
        o_ref[0, :, cs] = x[:, cs] + y * sc_ref[:, cs]


def _pool_layer(x, g, w, scale):
    B, S, D = x.shape
    tile = TOKEN_TILE
    nt = S // tile
    hb = tile // POOL_HALO
    nhb = S // POOL_HALO
    kern = functools.partial(_pool_kernel, seq_len=S, tile=tile)
    return pl.pallas_call(
        kern,
        grid=(B, nt),
        in_specs=[
            pl.BlockSpec((1, tile, D), lambda b, i: (b, i, 0)),
            pl.BlockSpec((1, POOL_HALO, D), lambda b, i: (b, jnp.maximum(i * hb - 1, 0), 0)),
            pl.BlockSpec((1, POOL_HALO, D), lambda b, i: (b, jnp.minimum((i + 1) * hb, nhb - 1), 0)),
            _resident((1, D)),
            _resident(w.shape),
            _resident((1, D)),
        ],
        out_specs=pl.BlockSpec((1, tile, D), lambda b, i: (b, i, 0)),
        out_shape=jax.ShapeDtypeStruct(x.shape, F32),
        scratch_shapes=[pltpu.VMEM((tile + 2 * POOL_HALO, D), F32)],
        compiler_params=_params(),
        name="pool_mixer",
    )(x, x, x, g, w, scale)


def _memkv_kernel(m_ref, g_ref, w_ref, k_ref, v_ref):
    mn = _rms(m_ref[0], g_ref[0]).astype(BF16)
    kv = _dot(mn, w_ref[0])
    k_ref[0, 0] = kv[:, :D_MODEL].astype(BF16)
    v_ref[0, 0] = kv[:, D_MODEL:].astype(BF16)


def _mem_kv(mem, g, w_kv):
    B = mem.shape[0]
    out = jax.ShapeDtypeStruct((DEPTH, B, N_MEM, D_MODEL), BF16)
    return pl.pallas_call(
        _memkv_kernel,
        grid=(DEPTH, B),
        in_specs=[
            pl.BlockSpec((1, N_MEM, D_MODEL), lambda l, b: (b, 0, 0)),
            pl.BlockSpec((1, 1, D_MODEL), lambda l, b: (l, 0, 0)),
            pl.BlockSpec((1, D_MODEL, 2 * D_MODEL), lambda l, b: (l, 0, 0)),
        ],
        out_specs=[
            pl.BlockSpec((1, 1, N_MEM, D_MODEL), lambda l, b: (l, b, 0, 0)),
            pl.BlockSpec((1, 1, N_MEM, D_MODEL), lambda l, b: (l, b, 0, 0)),
        ],
        out_shape=[out, out],
        compiler_params=_params(),
        name="mem_kv",
    )(mem, g, w_kv)


def _xattn_kernel(x_ref, g_ref, wq_ref, k_ref, v_ref, wo_ref, o_ref):
    x = x_ref[...]
    hn = _rms(x, g_ref[...]).astype(BF16)
    q = (_dot(hn, wq_ref[...]) * (X_HEAD_DIM ** -0.5)).astype(BF16)
    heads = []
    for h in range(X_HEADS):
        cs = slice(h * X_HEAD_DIM, (h + 1) * X_HEAD_DIM)
        s = _dot_nt(q[:, cs], k_ref[0, 0, :, cs])
        m = jnp.max(s, axis=-1, keepdims=True)
        e = jnp.exp(s - m)
        p = e / jnp.sum(e, axis=-1, keepdims=True)
        heads.append(_dot(p.astype(BF16), v_ref[0, 0, :, cs]).astype(BF16))
    o = jnp.concatenate(heads, axis=-1)
    o_ref[...] = x + _dot(o, wo_ref[...])


def _xattn_layer(x2d, seq_len, g, wq, k, v, wo, layer):
    N, D = x2d.shape
    tile = TOKEN_TILE
    tps = seq_len // tile
    kv_spec = pl.BlockSpec((1, 1, N_MEM, D), lambda i: (layer, i // tps, 0, 0))
    return pl.pallas_call(
        _xattn_kernel,
        grid=(N // tile,),
        in_specs=[
            pl.BlockSpec((tile, D), lambda i: (i, 0)),
            _resident((1, D)),
            _resident((D, D)),
            kv_spec,
            kv_spec,
            _resident((D, D)),
        ],
        out_specs=pl.BlockSpec((tile, D), lambda i: (i, 0)),
        out_shape=jax.ShapeDtypeStruct((N, D), F32),
        compiler_params=_params(),
        name="mem_xattn",
    )(x2d, g, wq, k, v, wo)


def _mlp_kernel(x_ref, g_ref, wu_ref, wd_ref, gf_ref, o_ref, *, final_norm):
    x = x_ref[...]
    hn = _rms(x, g_ref[...]).astype(BF16)
    acc = x
    for c in range(D_FF // FF_CHUNK):
        cs = slice(c * FF_CHUNK, (c + 1) * FF_CHUNK)
        u = jnp.maximum(_dot(hn, wu_ref[:, cs]), 0.0)
        acc = acc + _dot((u * u).astype(BF16), wd_ref[cs, :])
    if final_norm:
        acc = _rms(acc, gf_ref[...])
    o_ref[...] = acc


def _mlp_layer(x2d, g, w_up, w_down, g_final, final_norm):
    N, D = x2d.shape
    tile = TOKEN_TILE
    kern = functools.partial(_mlp_kernel, final_norm=final_norm)
    return pl.pallas_call(
        kern,
        grid=(N // tile,),
        in_specs=[
            pl.BlockSpec((tile, D), lambda i: (i, 0)),
            _resident((1, D)),
            _resident((D, D_FF)),
            _resident((D_FF, D)),
            _resident((1, D)),
        ],
        out_specs=pl.BlockSpec((tile, D), lambda i: (i, 0)),
        out_shape=jax.ShapeDtypeStruct((N, D), F32),
        compiler_params=_params(),
        name="sq_relu_mlp",
    )(x2d, g, w_up, w_down, g_final)


def _head_perm():
    idx = []
    for j in range(GQA_GROUP):
        for kh in range(N_KV):
            h = GQA_GROUP * kh + j
            idx.extend(range(h * HEAD_DIM, (h + 1) * HEAD_DIM))
    return jnp.asarray(idx, dtype=jnp.int32)


def _rope_tables(seq_len):
    inv_freq = ROPE_THETA ** (-jnp.arange(0, HEAD_DIM, 2, dtype=F32) / HEAD_DIM)
    ang = jnp.arange(seq_len, dtype=F32)[:, None] * inv_freq[None, :]
    cos, sin = jnp.cos(ang), jnp.sin(ang)
    zero = jnp.zeros_like(sin)
    reps = LANES // HEAD_DIM
    cos_t = jnp.tile(jnp.concatenate([cos, cos], axis=-1), (1, reps))
    sa_t = jnp.tile(jnp.concatenate([-sin, zero], axis=-1), (1, reps))
    sb_t = jnp.tile(jnp.concatenate([zero, sin], axis=-1), (1, reps))
    return cos_t, sa_t, sb_t, cos.T, sin.T


def _qkv_kernel(x_ref, g_ref, wq_ref, wkt_ref, wv_ref, cos_ref, sa_ref, sb_ref, cost_ref, sint_ref,
                q_ref, kt_ref, v_ref):
    hn = _rms(x_ref[...], g_ref[...]).astype(BF16)
    half = HEAD_DIM // 2

    q = _dot(hn, wq_ref[...])
    cos, sa, sb = cos_ref[...], sa_ref[...], sb_ref[...]
    for cb in range(q.shape[1] // LANES):
        cs = slice(cb * LANES, (cb + 1) * LANES)
        xb = q[:, cs]
        r = xb * cos + pltpu.roll(xb, LANES - half, 1) * sa + pltpu.roll(xb, half, 1) * sb
        q_ref[:, cs] = (r * (HEAD_DIM ** -0.5)).astype(BF16)

    kt = _dot_nt(wkt_ref[...], hn)
    cost, sint = cost_ref[...], sint_ref[...]
    for kh in range(N_KV):
        a = kt[kh * HEAD_DIM: kh * HEAD_DIM + half]
        b = kt[kh * HEAD_DIM + half: (kh + 1) * HEAD_DIM]
        kt_ref[0, kh * HEAD_DIM: kh * HEAD_DIM + half, :] = (a * cost - b * sint).astype(BF16)
        kt_ref[0, kh * HEAD_DIM + half: (kh + 1) * HEAD_DIM, :] = (b * cost + a * sint).astype(BF16)

    v_ref[...] = _dot(hn, wv_ref[...]).astype(BF16)


def _qkv_layer(x2d, seq_len, g, wq, wkt, wv, tables):
    N, D = x2d.shape
    tile = TOKEN_TILE
    tps = seq_len // tile
    B = N // seq_len
    nq = wq.shape[1]
    cos_t, sa_t, sb_t, cost, sint = tables
    tab_spec = pl.BlockSpec((tile, LANES), lambda i: (i % tps, 0))
    tabt_spec = pl.BlockSpec((HEAD_DIM // 2, tile), lambda i: (0, i % tps))
    return pl.pallas_call(
        _qkv_kernel,
        grid=(N // tile,),
        in_specs=[
            pl.BlockSpec((tile, D), lambda i: (i, 0)),
            _resident((1, D)),
            _resident(wq.shape),
            _resident(wkt.shape),
            _resident(wv.shape),
            tab_spec, tab_spec, tab_spec,
            tabt_spec, tabt_spec,
        ],
        out_specs=[
            pl.BlockSpec((tile, nq), lambda i: (i, 0)),
            pl.BlockSpec((1, KV_WIDTH, tile), lambda i: (i // tps, 0, i % tps)),
            pl.BlockSpec((tile, KV_WIDTH), lambda i: (i, 0)),
        ],
        out_shape=[
            jax.ShapeDtypeStruct((N, nq), BF16),
            jax.ShapeDtypeStruct((B, KV_WIDTH, seq_len), BF16),
            jax.ShapeDtypeStruct((N, KV_WIDTH), BF16),
        ],
        compiler_params=_params(),
        name="qkv_rope",
    )(x2d, g, wq, wkt, wv, cos_t, sa_t, sb_t, cost, sint)


def _winattn_kernel(sink_ref, x_ref, q_ref, kp_ref, kc_ref, kn_ref, vp_ref, vc_ref, vn_ref, wo_ref, o_ref):
    i = pl.program_id(1)
    last = pl.num_programs(1) - 1
    slab = KV_WIDTH

    q = q_ref[0]
    qst = jnp.concatenate([q[:, j * slab:(j + 1) * slab] for j in range(GQA_GROUP)], axis=0)
    kt = jnp.concatenate([kp_ref[0], kc_ref[0], kn_ref[0]], axis=1)
    zero_k = jnp.zeros((HEAD_DIM, N_KEYS), BF16)
    bdk = jnp.concatenate(
        [jnp.concatenate([kt[kh * HEAD_DIM:(kh + 1) * HEAD_DIM] if r == kh else zero_k
                          for r in range(N_KV)], axis=0)
         for kh in range(N_KV)], axis=1)
    s = _dot(qst, bdk)

    rows = GQA_GROUP * Q_BLOCK
    r = lax.broadcasted_iota(jnp.int32, (rows, N_KEYS), 0) & (Q_BLOCK - 1)
    c = lax.broadcasted_iota(jnp.int32, (rows, N_KEYS), 1)
    mask = jnp.abs(c - WINDOW - r) <= WINDOW
    mask = mask & ((c >= WINDOW) | (i > 0)) & ((c < WINDOW + Q_BLOCK) | (i < last))

    probs = []
    for kh in range(N_KV):
        sk = jnp.where(mask, s[:, kh * N_KEYS:(kh + 1) * N_KEYS], -1e30)
        sink = jnp.concatenate(
            [jnp.full((Q_BLOCK, 1), sink_ref[GQA_GROUP * kh + j], F32) for j in range(GQA_GROUP)], axis=0)
        m = jnp.maximum(jnp.max(sk, axis=-1, keepdims=True), sink)
        e = jnp.exp(sk - m)
        denom = jnp.sum(e, axis=-1, keepdims=True) + jnp.exp(sink - m)
        probs.append((e * (1.0 / denom)).astype(BF16))
    p = jnp.concatenate(probs, axis=1)

    v = jnp.concatenate([vp_ref[0], vc_ref[0], vn_ref[0]], axis=0)
    lane_head = lax.shift_right_logical(
        lax.broadcasted_iota(jnp.int32, (N_KEYS, KV_WIDTH), 1), HEAD_DIM.bit_length() - 1)
    bdv = jnp.concatenate([jnp.where(lane_head == kh, v, jnp.zeros_like(v)) for kh in range(N_KV)], axis=0)
    o = _dot(p, bdv).astype(BF16)
    a = jnp.concatenate([o[j * Q_BLOCK:(j + 1) * Q_BLOCK] for j in range(GQA_GROUP)], axis=1)
    o_ref[0] = x_ref[0] + _dot(a, wo_ref[...])


def _winattn_layer(x, q, kt, v, sinks, wo):
    B, S, D = x.shape
    nq = q.shape[-1]
    nb = S // Q_BLOCK
    x_spec = pl.BlockSpec((1, Q_BLOCK, D), lambda b, i: (b, i, 0))
    q_spec = pl.BlockSpec((1, Q_BLOCK, nq), lambda b, i: (b, i, 0))

    def kt_spec(f):
        return pl.BlockSpec((1, KV_WIDTH, Q_BLOCK), lambda b, i: (b, 0, f(i)))

    def v_spec(f):
        return pl.BlockSpec((1, Q_BLOCK, KV_WIDTH), lambda b, i: (b, f(i), 0))

    prev = lambda i: jnp.maximum(i - 1, 0)
    cur = lambda i: i
    nxt = lambda i: jnp.minimum(i + 1, nb - 1)
    return pl.pallas_call(
        _winattn_kernel,
        grid=(B, nb),
        in_specs=[
            pl.BlockSpec(memory_space=pltpu.SMEM),
            x_spec, q_spec,
            kt_spec(prev), kt_spec(cur), kt_spec(nxt),
            v_spec(prev), v_spec(cur), v_spec(nxt),
            _resident(wo.shape),
        ],
        out_specs=x_spec,
        out_shape=jax.ShapeDtypeStruct((B, S, D), F32),
        compiler_params=_params(),
        name="window_attn",
    )(sinks, x, q, kt, kt, kt, v, v, v, wo)


def _trunk(x, mem, p):
    B, S, D = x.shape
    N = B * S
    k_mem, v_mem = _mem_kv(mem, p["norm_mem"], p["x_wkv"])
    tables = _rope_tables(S)

    def row(a, i):
        return a[i].reshape(1, -1)

    x = _pool_layer(x, row(p["norm_mix"], 0), p["pool_w"][0], row(p["pool_scale"], 0))
    x2 = x.reshape(N, D)
    for i in range(DEPTH):
        if i % 2 == 1:
            j = i // 2
            q, kt, v = _qkv_layer(x2, S, row(p["norm_mix"], i), p["attn_wq"][j], p["attn_wkt"][j],
                                  p["attn_wv"][j], tables)
            x2 = _winattn_layer(x2.reshape(B, S, D), q.reshape(B, S, -1), kt, v.reshape(B, S, -1),
                                p["attn_sink"][j], p["attn_o"][j]).reshape(N, D)
        x2 = _xattn_layer(x2, S, row(p["norm_x"], i), p["x_wq"][i], k_mem, v_mem, p["x_wo"][i], i)
        x2 = _mlp_layer(x2, row(p["norm_mlp"], i), p["w_up"][i], p["w_down"][i],
                        p["norm_final"].reshape(1, -1), final_norm=(i == DEPTH - 1))
    return x2.reshape(B, S, D)


def kernel(x_prompt, x_sample, mem_prompt, mem_sample, norm_mix, pool_w, pool_scale, attn_qkv, attn_o,
           attn_sink, norm_x, norm_mem, x_wq, x_wkv, x_wo, norm_mlp, w_up, w_down, norm_final):
    nq, nk = N_HEADS * HEAD_DIM, N_KV * HEAD_DIM
    perm = _head_perm()
    p = dict(
        norm_mix=norm_mix, pool_scale=pool_scale, attn_sink=attn_sink, norm_x=norm_x,
        norm_mem=norm_mem.reshape(DEPTH, 1, D_MODEL), norm_mlp=norm_mlp, norm_final=norm_final,
        pool_w=pool_w.astype(BF16),
        attn_wq=attn_qkv[:, :, :nq][:, :, perm].astype(BF16),
        attn_wkt=jnp.swapaxes(attn_qkv[:, :, nq:nq + nk], 1, 2).astype(BF16),
        attn_wv=attn_qkv[:, :, nq + nk:].astype(BF16),
        attn_o=attn_o[:, perm, :].astype(BF16),
        x_wq=x_wq.astype(BF16), x_wkv=x_wkv.astype(BF16), x_wo=x_wo.astype(BF16),
        w_up=w_up.astype(BF16), w_down=w_down.astype(BF16),
    )
    return (_trunk(x_prompt, mem_prompt, p), _trunk(x_sample, mem_sample, p))
```

```python
import functools

import jax
import jax.numpy as jnp
from jax import lax
from jax.experimental import pallas as pl
from jax.experimental.pallas import tpu as pltpu

D_MODEL = 1024
DEPTH = 2
EPS = 1e-6
POOL_WINDOWS = (2, 4, 8, 16)
GROUP = D_MODEL // len(POOL_WINDOWS)
HEAD_DIM = 64
N_HEADS = 16
N_KV = 4
GQA_GROUP = N_HEADS // N_KV
WINDOW = 128
ROPE_THETA = 10000.0
N_MEM = 256
X_HEADS = 4
X_HEAD_DIM = D_MODEL // X_HEADS
D_FF = 4 * D_MODEL

LANES = 128
POOL_HALO = 16
VMEM_LIMIT = 56 * 1024 * 1024

TOKEN_TILE = 512
FF_CHUNK = 1024
Q_BLOCK = 128
ATTN_BLOCKS = 2
N_KEYS = Q_BLOCK + 2 * WINDOW
KV_WIDTH = N_KV * HEAD_DIM

BF16 = jnp.bfloat16
F32 = jnp.float32


def _params():
    return pltpu.CompilerParams(vmem_limit_bytes=VMEM_LIMIT)


def _resident(shape):
    zeros = (0,) * len(shape)
    return pl.BlockSpec(shape, lambda *_: zeros, pipeline_mode=pl.Buffered(1))


def _rms(x, g):
    ms = jnp.mean(x * x, axis=-1, keepdims=True)
    return (x * lax.rsqrt(ms + EPS)) * g


def _dot(a, b):
    return jnp.dot(a, b, preferred_element_type=F32)


def _dot_nt(a, b):
    return lax.dot_general(a, b, (((1,), (1,)), ((), ())), preferred_element_type=F32)


def _pool_kernel(xc_ref, xp_ref, xn_ref, g_ref, w_ref, sc_ref, o_ref, hbuf, *, seq_len, tile):
    i = pl.program_id(1)
    last = pl.num_programs(1) - 1
    g = g_ref[...]
    x = xc_ref[0]
    hn = _rms(x, g)
    hp = _rms(xp_ref[0], g) * jnp.where(i > 0, 1.0, 0.0)
    hx = _rms(xn_ref[0], g) * jnp.where(i < last, 1.0, 0.0)
    hbuf[0:POOL_HALO, :] = hp
    hbuf[POOL_HALO:POOL_HALO + tile, :] = hn
    hbuf[POOL_HALO + tile:, :] = hx

    t = i * tile + lax.broadcasted_iota(jnp.int32, (tile, 1), 0)
    for gi, w in enumerate(POOL_WINDOWS):
        cs = slice(gi * GROUP, (gi + 1) * GROUP)
        total = hbuf[pl.ds(POOL_HALO - w // 2, tile), cs]
        for j in range(-w // 2 + 1, w // 2):
            total = total + hbuf[pl.ds(POOL_HALO + j, tile), cs]
        lo = jnp.maximum(t - w // 2, 0)
        hi = jnp.minimum(t + w // 2, seq_len)
        count = (hi - lo).astype(F32)
        diff = total / count - hn[:, cs]
        y = _dot(diff.astype(BF16), w_ref[gi])
        o_ref[0, :, cs] = x[:, cs] + y * sc_ref[:, cs]


def _pool_layer(x, g, w, scale):
    B, S, D = x.shape
    tile = TOKEN_TILE
    nt = S // tile
    hb = tile // POOL_HALO
    nhb = S // POOL_HALO
    kern = functools.partial(_pool_kernel, seq_len=S, tile=tile)
    return pl.pallas_call(
        kern,
        grid=(B, nt),
        in_specs=[
            pl.BlockSpec((1, tile, D), lambda b, i: (b, i, 0)),
            pl.BlockSpec((1, POOL_HALO, D), lambda b, i: (b, jnp.maximum(i * hb - 1, 0), 0)),
            pl.BlockSpec((1, POOL_HALO, D), lambda b, i: (b, jnp.minimum((i + 1) * hb, nhb - 1), 0)),
            _resident((1, D)),
            _resident(w.shape),
            _resident((1, D)),
        ],
        out_specs=pl.BlockSpec((1, tile, D), lambda b, i: (b, i, 0)),
        out_shape=jax.ShapeDtypeStruct(x.shape, F32),
        scratch_shapes=[pltpu.VMEM((tile + 2 * POOL_HALO, D), F32)],
        compiler_params=_params(),
        name="pool_mixer",
    )(x, x, x, g, w, scale)


def _memkv_kernel(m_ref, g_ref, w_ref, k_ref, v_ref):
    mn = _rms(m_ref[0], g_ref[0]).astype(BF16)
    kv = _dot(mn, w_ref[0])
    k_ref[0, 0] = kv[:, :D_MODEL].astype(BF16)
    v_ref[0, 0] = kv[:, D_MODEL:].astype(BF16)


def _mem_kv(mem, g, w_kv):
    B = mem.shape[0]
    out = jax.ShapeDtypeStruct((DEPTH, B, N_MEM, D_MODEL), BF16)
    return pl.pallas_call(
        _memkv_kernel,
        grid=(DEPTH, B),
        in_specs=[
            pl.BlockSpec((1, N_MEM, D_MODEL), lambda l, b: (b, 0, 0)),
            pl.BlockSpec((1, 1, D_MODEL), lambda l, b: (l, 0, 0)),
            pl.BlockSpec((1, D_MODEL, 2 * D_MODEL), lambda l, b: (l, 0, 0)),
        ],
        out_specs=[
            pl.BlockSpec((1, 1, N_MEM, D_MODEL), lambda l, b: (l, b, 0, 0)),
            pl.BlockSpec((1, 1, N_MEM, D_MODEL), lambda l, b: (l, b, 0, 0)),
        ],
        out_shape=[out, out],
        compiler_params=_params(),
        name="mem_kv",
    )(mem, g, w_kv)


def _xattn_kernel(x_ref, g_ref, wq_ref, k_ref, v_ref, wo_ref, o_ref):
    x = x_ref[...]
    hn = _rms(x, g_ref[...]).astype(BF16)
    q = (_dot(hn, wq_ref[...]) * (X_HEAD_DIM ** -0.5)).astype(BF16)
    heads = []
    for h in range(X_HEADS):
        cs = slice(h * X_HEAD_DIM, (h + 1) * X_HEAD_DIM)
        s = _dot_nt(q[:, cs], k_ref[0, 0, :, cs])
        m = jnp.max(s, axis=-1, keepdims=True)
        e = jnp.exp(s - m)
        p = e / jnp.sum(e, axis=-1, keepdims=True)
        heads.append(_dot(p.astype(BF16), v_ref[0, 0, :, cs]).astype(BF16))
    o = jnp.concatenate(heads, axis=-1)
    o_ref[...] = x + _dot(o, wo_ref[...])


def _xattn_layer(x2d, seq_len, g, wq, k, v, wo, layer):
    N, D = x2d.shape
    tile = TOKEN_TILE
    tps = seq_len // tile
    kv_spec = pl.BlockSpec((1, 1, N_MEM, D), lambda i: (layer, i // tps, 0, 0))
    return pl.pallas_call(
        _xattn_kernel,
        grid=(N // tile,),
        in_specs=[
            pl.BlockSpec((tile, D), lambda i: (i, 0)),
            _resident((1, D)),
            _resident((D, D)),
            kv_spec,
            kv_spec,
            _resident((D, D)),
        ],
        out_specs=pl.BlockSpec((tile, D), lambda i: (i, 0)),
        out_shape=jax.ShapeDtypeStruct((N, D), F32),
        compiler_params=_params(),
        name="mem_xattn",
    )(x2d, g, wq, k, v, wo)


def _mlp_kernel(x_ref, g_ref, wu_ref, wd_ref, gf_ref, o_ref, *, final_norm):
    x = x_ref[...]
    hn = _rms(x, g_ref[...]).astype(BF16)
    acc = x
    for c in range(D_FF // FF_CHUNK):
        cs = slice(c * FF_CHUNK, (c + 1) * FF_CHUNK)
        u = jnp.maximum(_dot(hn, wu_ref[:, cs]), 0.0)
        acc = acc + _dot((u * u).astype(BF16), wd_ref[cs, :])
    if final_norm:
        acc = _rms(acc, gf_ref[...])
    o_ref[...] = acc


def _mlp_layer(x2d, g, w_up, w_down, g_final, final_norm):
    N, D = x2d.shape
    tile = TOKEN_TILE
    kern = functools.partial(_mlp_kernel, final_norm=final_norm)
    return pl.pallas_call(
        kern,
        grid=(N // tile,),
        in_specs=[
            pl.BlockSpec((tile, D), lambda i: (i, 0)),
            _resident((1, D)),
            _resident((D, D_FF)),
            _resident((D_FF, D)),
            _resident((1, D)),
        ],
        out_specs=pl.BlockSpec((tile, D), lambda i: (i, 0)),
        out_shape=jax.ShapeDtypeStruct((N, D), F32),
        compiler_params=_params(),
        name="sq_relu_mlp",
    )(x2d, g, w_up, w_down, g_final)


def _head_perm():
    idx = []
    for j in range(GQA_GROUP):
        for kh in range(N_KV):
            h = GQA_GROUP * kh + j
            idx.extend(range(h * HEAD_DIM, (h + 1) * HEAD_DIM))
    return jnp.asarray(idx, dtype=jnp.int32)


def _rope_tables(seq_len):
    inv_freq = ROPE_THETA ** (-jnp.arange(0, HEAD_DIM, 2, dtype=F32) / HEAD_DIM)
    ang = jnp.arange(seq_len, dtype=F32)[:, None] * inv_freq[None, :]
    cos, sin = jnp.cos(ang), jnp.sin(ang)
    zero = jnp.zeros_like(sin)
    reps = LANES // HEAD_DIM
    cos_t = jnp.tile(jnp.concatenate([cos, cos], axis=-1), (1, reps))
    sa_t = jnp.tile(jnp.concatenate([-sin, zero], axis=-1), (1, reps))
    sb_t = jnp.tile(jnp.concatenate([zero, sin], axis=-1), (1, reps))
    return cos_t, sa_t, sb_t, cos.T, sin.T


def _qkv_kernel(x_ref, g_ref, wqvt_ref, wk_ref, cos_ref, sa_ref, sb_ref, cost_ref, sint_ref,
                qt_ref, k_ref, vt_ref):
    hn = _rms(x_ref[...], g_ref[...]).astype(BF16)
    half = HEAD_DIM // 2
    nq = N_HEADS * HEAD_DIM

    qvt = _dot_nt(wqvt_ref[...], hn)
    cost, sint = cost_ref[...], sint_ref[...]
    scale = HEAD_DIM ** -0.5
    for h in range(N_HEADS):
        a = qvt[h * HEAD_DIM: h * HEAD_DIM + half]
        b = qvt[h * HEAD_DIM + half: (h + 1) * HEAD_DIM]
        qt_ref[0, h * HEAD_DIM: h * HEAD_DIM + half, :] = ((a * cost - b * sint) * scale).astype(BF16)
        qt_ref[0, h * HEAD_DIM + half: (h + 1) * HEAD_DIM, :] = ((b * cost + a * sint) * scale).astype(BF16)
    vt_ref[0] = qvt[nq:].astype(BF16)

    k = _dot(hn, wk_ref[...])
    cos, sa, sb = cos_ref[...], sa_ref[...], sb_ref[...]
    for cb in range(KV_WIDTH // LANES):
        cs = slice(cb * LANES, (cb + 1) * LANES)
        xb = k[:, cs]
        r = xb * cos + pltpu.roll(xb, LANES - half, 1) * sa + pltpu.roll(xb, half, 1) * sb
        k_ref[:, cs] = r.astype(BF16)


def _qkv_layer(x2d, seq_len, g, wqvt, wk, tables):
    N, D = x2d.shape
    tile = TOKEN_TILE
    tps = seq_len // tile
    B = N // seq_len
    nq = N_HEADS * HEAD_DIM
    cos_t, sa_t, sb_t, cost, sint = tables
    tab_spec = pl.BlockSpec((tile, LANES), lambda i: (i % tps, 0))
    tabt_spec = pl.BlockSpec((HEAD_DIM // 2, tile), lambda i: (0, i % tps))
    return pl.pallas_call(
        _qkv_kernel,
        grid=(N // tile,),
        in_specs=[
            pl.BlockSpec((tile, D), lambda i: (i, 0)),
            _resident((1, D)),
            _resident(wqvt.shape),
            _resident(wk.shape),
            tab_spec, tab_spec, tab_spec,
            tabt_spec, tabt_spec,
        ],
        out_specs=[
            pl.BlockSpec((1, nq, tile), lambda i: (i // tps, 0, i % tps)),
            pl.BlockSpec((tile, KV_WIDTH), lambda i: (i, 0)),
            pl.BlockSpec((1, KV_WIDTH, tile), lambda i: (i // tps, 0, i % tps)),
        ],
        out_shape=[
            jax.ShapeDtypeStruct((B, nq, seq_len), BF16),
            jax.ShapeDtypeStruct((N, KV_WIDTH), BF16),
            jax.ShapeDtypeStruct((B, KV_WIDTH, seq_len), BF16),
        ],
        compiler_params=_params(),
        name="qkv_rope",
    )(x2d, g, wqvt, wk, cos_t, sa_t, sb_t, cost, sint)


def _attend_block(qt, k, vt, sink_ref, prev_valid, next_valid):
    slab = KV_WIDTH
    qst = jnp.concatenate([qt[j * slab:(j + 1) * slab] for j in range(GQA_GROUP)], axis=1)
    lane_head = lax.shift_right_logical(
        lax.broadcasted_iota(jnp.int32, k.shape, 1), HEAD_DIM.bit_length() - 1)
    kbd = jnp.concatenate([jnp.where(lane_head == kh, k, jnp.zeros_like(k)) for kh in range(N_KV)], axis=0)
    s = _dot(kbd, qst)

    lanes = GQA_GROUP * Q_BLOCK
    r = lax.broadcasted_iota(jnp.int32, (WINDOW, lanes), 0)
    c = lax.broadcasted_iota(jnp.int32, (WINDOW, lanes), 1) & (Q_BLOCK - 1)
    prev_ok = (r >= c) & prev_valid
    next_ok = (r <= c) & next_valid

    probs = []
    for kh in range(N_KV):
        base = kh * N_KEYS
        sk = jnp.concatenate([
            jnp.where(prev_ok, s[base:base + WINDOW], -1e30),
            s[base + WINDOW:base + WINDOW + Q_BLOCK],
            jnp.where(next_ok, s[base + WINDOW + Q_BLOCK:base + N_KEYS], -1e30)], axis=0)
        sink = jnp.concatenate(
            [jnp.full((1, Q_BLOCK), sink_ref[GQA_GROUP * kh + j], F32) for j in range(GQA_GROUP)], axis=1)
        m = jnp.maximum(jnp.max(sk, axis=0, keepdims=True), sink)
        e = jnp.exp(sk - m)
        denom = jnp.sum(e, axis=0, keepdims=True) + jnp.exp(sink - m)
        probs.append((e * (1.0 / denom)).astype(BF16))
    pt = jnp.concatenate(probs, axis=0)

    zero_v = jnp.zeros((HEAD_DIM, N_KEYS), BF16)
    vbd = jnp.concatenate(
        [jnp.concatenate([vt[kh * HEAD_DIM:(kh + 1) * HEAD_DIM] if rr == kh else zero_v
                          for rr in range(N_KV)], axis=0)
         for kh in range(N_KV)], axis=1)
    ot = _dot(vbd, pt)
    return jnp.concatenate([ot[:, j * Q_BLOCK:(j + 1) * Q_BLOCK] for j in range(GQA_GROUP)], axis=0)


def _winattn_kernel(sink_ref, x_ref, qt_ref, kp_ref, kc_ref, kn_ref, vp_ref, vc_ref, vn_ref, wo_ref, o_ref,
                    *, blocks):
    i = pl.program_id(1)
    last = pl.num_programs(1) - 1
    qt = qt_ref[0]
    k_all = jnp.concatenate([kp_ref[0], kc_ref[0], kn_ref[0]], axis=0)
    vt_all = jnp.concatenate([vp_ref[0], vc_ref[0], vn_ref[0]], axis=1)
    outs = []
    for blk in range(blocks):
        lo = blk * Q_BLOCK
        prev_valid = (i > 0) if blk == 0 else True
        next_valid = (i < last) if blk == blocks - 1 else True
        outs.append(_attend_block(qt[:, lo:lo + Q_BLOCK], k_all[lo:lo + N_KEYS], vt_all[:, lo:lo + N_KEYS],
                                  sink_ref, prev_valid, next_valid))
    at = jnp.concatenate(outs, axis=1).astype(BF16)
    proj = lax.dot_general(at, wo_ref[...], (((0,), (0,)), ((), ())), preferred_element_type=F32)
    o_ref[0] = x_ref[0] + proj


def _winattn_layer(x, qt, k, vt, sinks, wo):
    B, S, D = x.shape
    nq = qt.shape[1]
    blocks = ATTN_BLOCKS
    rows = blocks * Q_BLOCK
    ns = S // rows
    nb = S // Q_BLOCK
    x_spec = pl.BlockSpec((1, rows, D), lambda b, i: (b, i, 0))
    qt_spec = pl.BlockSpec((1, nq, rows), lambda b, i: (b, 0, i))
    prev = lambda i: jnp.maximum(i * blocks - 1, 0)
    nxt = lambda i: jnp.minimum((i + 1) * blocks, nb - 1)
    kern = functools.partial(_winattn_kernel, blocks=blocks)
    return pl.pallas_call(
        kern,
        grid=(B, ns),
        in_specs=[
            pl.BlockSpec(memory_space=pltpu.SMEM),
            x_spec, qt_spec,
            pl.BlockSpec((1, Q_BLOCK, KV_WIDTH), lambda b, i: (b, prev(i), 0)),
            pl.BlockSpec((1, rows, KV_WIDTH), lambda b, i: (b, i, 0)),
            pl.BlockSpec((1, Q_BLOCK, KV_WIDTH), lambda b, i: (b, nxt(i), 0)),
            pl.BlockSpec((1, KV_WIDTH, Q_BLOCK), lambda b, i: (b, 0, prev(i))),
            pl.BlockSpec((1, KV_WIDTH, rows), lambda b, i: (b, 0, i)),
            pl.BlockSpec((1, KV_WIDTH, Q_BLOCK), lambda b, i: (b, 0, nxt(i))),
            _resident(wo.shape),
        ],
        out_specs=x_spec,
        out_shape=jax.ShapeDtypeStruct((B, S, D), F32),
        compiler_params=_params(),
        name="window_attn",
    )(sinks, x, qt, k, k, k, vt, vt, vt, wo)


def _trunk(x, mem, p):
    B, S, D = x.shape
    N = B * S
    k_mem, v_mem = _mem_kv(mem, p["norm_mem"], p["x_wkv"])
    tables = _rope_tables(S)

    def row(a, i):
        return a[i].reshape(1, -1)

    x = _pool_layer(x, row(p["norm_mix"], 0), p["pool_w"][0], row(p["pool_scale"], 0))
    x2 = x.reshape(N, D)
    for i in range(DEPTH):
        if i % 2 == 1:
            j = i // 2
            qt, k, vt = _qkv_layer(x2, S, row(p["norm_mix"], i), p["attn_wqvt"][j], p["attn_wk"][j], tables)
            x2 = _winattn_layer(x2.reshape(B, S, D), qt, k.reshape(B, S, -1), vt,
                                p["attn_sink"][j], p["attn_o"][j]).reshape(N, D)
        x2 = _xattn_layer(x2, S, row(p["norm_x"], i), p["x_wq"][i], k_mem, v_mem, p["x_wo"][i], i)
        x2 = _mlp_layer(x2, row(p["norm_mlp"], i), p["w_up"][i], p["w_down"][i],
                        p["norm_final"].reshape(1, -1), final_norm=(i == DEPTH - 1))
    return x2.reshape(B, S, D)


def kernel(x_prompt, x_sample, mem_prompt, mem_sample, norm_mix, pool_w, pool_scale, attn_qkv, attn_o,
           attn_sink, norm_x, norm_mem, x_wq, x_wkv, x_wo, norm_mlp, w_up, w_down, norm_final):
    nq, nk = N_HEADS * HEAD_DIM, N_KV * HEAD_DIM
    perm = _head_perm()
    p = dict(
        norm_mix=norm_mix, pool_scale=pool_scale, attn_sink=attn_sink, norm_x=norm_x,
        norm_mem=norm_mem.reshape(DEPTH, 1, D_MODEL), norm_mlp=norm_mlp, norm_final=norm_final,
        pool_w=pool_w.astype(BF16),
        attn_wqvt=jnp.swapaxes(
            jnp.concatenate([attn_qkv[:, :, :nq][:, :, perm], attn_qkv[:, :, nq + nk:]], axis=2), 1, 2).astype(BF16),
        attn_wk=attn_qkv[:, :, nq:nq + nk].astype(BF16),
        attn_o=attn_o[:, perm, :].astype(BF16),
        x_wq=x_wq.astype(BF16), x_wkv=x_wkv.astype(BF16), x_wo=x_wo.astype(BF16),
        w_up=w_up.astype(BF16), w_down=w_down.astype(BF16),
    )
    return (_trunk(x_prompt, mem_prompt, p), _trunk(x_sample, mem_sample, p))
```

```python
import functools

import jax
import jax.numpy as jnp
from jax import lax
from jax.experimental import pallas as pl
from jax.experimental.pallas import tpu as pltpu

D_MODEL = 1024
DEPTH = 2
EPS = 1e-6
POOL_WINDOWS = (2, 4, 8, 16)
GROUP = D_MODEL // len(POOL_WINDOWS)
HEAD_DIM = 64
N_HEADS = 16
N_KV = 4
GQA_GROUP = N_HEADS // N_KV
WINDOW = 128
ROPE_THETA = 10000.0
N_MEM = 256
X_HEADS = 4
X_HEAD_DIM = D_MODEL // X_HEADS
D_FF = 4 * D_MODEL

LANES = 128
SUBLANES = 8
POOL_HALO = 16
POOL_PAD = SUBLANES
VMEM_LIMIT = 56 * 1024 * 1024

TOKEN_TILE = 512
FF_CHUNK = 1024
Q_BLOCK = 128
ATTN_BLOCKS = 2
N_KEYS = Q_BLOCK + 2 * WINDOW
KV_WIDTH = N_KV * HEAD_DIM

BF16 = jnp.bfloat16
F32 = jnp.float32


def _params():
    return pltpu.CompilerParams(vmem_limit_bytes=VMEM_LIMIT)


def _resident(shape):
    zeros = (0,) * len(shape)
    return pl.BlockSpec(shape, lambda *_: zeros, pipeline_mode=pl.Buffered(1))


def _rms(x, g):
    ms = jnp.mean(x * x, axis=-1, keepdims=True)
    return (x * lax.rsqrt(ms + EPS)) * g


def _dot(a, b):
    return jnp.dot(a, b, preferred_element_type=F32)


def _dot_nt(a, b):
    return lax.dot_general(a, b, (((1,), (1,)), ((), ())), preferred_element_type=F32)


def _pool_kernel(xc_ref, xp_ref, xn_ref, g_ref, w_ref, sc_ref, o_ref, hbuf, s1, s2, *, seq_len, tile):
    i = pl.program_id(1)
    last = pl.num_programs(1) - 1
    g = g_ref[...]
    x = xc_ref[0]
    hn = _rms(x, g)
    hp = _rms(xp_ref[0], g) * jnp.where(i > 0, 1.0, 0.0)
    hx = _rms(xn_ref[0], g) * jnp.where(i < last, 1.0, 0.0)
    H = POOL_HALO
    hbuf[0:H, :] = hp
    hbuf[H:H + tile, :] = hn
    hbuf[H + tile:H + tile + H, :] = hx
    hbuf[H + tile + H:, :] = jnp.zeros((POOL_PAD, hbuf.shape[1]), F32)

    n8 = tile + H
    n4 = n8 + POOL_PAD
    n2 = n4 + POOL_PAD
    t = i * tile + lax.broadcasted_iota(jnp.int32, (tile, 1), 0)
    for gi, w in enumerate(POOL_WINDOWS):
        cs = slice(gi * GROUP, (gi + 1) * GROUP)
        if w == 2:
            total = hbuf[pl.ds(H - 1, tile), cs] + hbuf[pl.ds(H, tile), cs]
        else:
            s1[0:n2, :] = hbuf[0:n2, cs] + hbuf[1:n2 + 1, cs]
            if w == 4:
                total = s1[pl.ds(H - 2, tile), :] + s1[pl.ds(H, tile), :]
            else:
                s2[0:n4, :] = s1[0:n4, :] + s1[2:n4 + 2, :]
                if w == 8:
                    total = s2[pl.ds(H - 4, tile), :] + s2[pl.ds(H, tile), :]
                else:
                    s1[0:n8, :] = s2[0:n8, :] + s2[4:n8 + 4, :]
                    total = s1[pl.ds(H - 8, tile), :] + s1[pl.ds(H, tile), :]
        lo = jnp.maximum(t - w // 2, 0)
        hi = jnp.minimum(t + w // 2, seq_len)
        count = (hi - lo).astype(F32)
        diff = total / count - hn[:, cs]
        y = _dot(diff.astype(BF16), w_ref[gi])
        o_ref[0, :, cs] = x[:, cs] + y * sc_ref[:, cs]


def _pool_layer(x, g, w, scale):
    B, S, D = x.shape
    tile = TOKEN_TILE
    nt = S // tile
    hb = tile // POOL_HALO
    nhb = S // POOL_HALO
    kern = functools.partial(_pool_kernel, seq_len=S, tile=tile)
    return pl.pallas_call(
        kern,
        grid=(B, nt),
        in_specs=[
            pl.BlockSpec((1, tile, D), lambda b, i: (b, i, 0)),
            pl.BlockSpec((1, POOL_HALO, D), lambda b, i: (b, jnp.maximum(i * hb - 1, 0), 0)),
            pl.BlockSpec((1, POOL_HALO, D), lambda b, i: (b, jnp.minimum((i + 1) * hb, nhb - 1), 0)),
            _resident((1, D)),
            _resident(w.shape),
            _resident((1, D)),
        ],
        out_specs=pl.BlockSpec((1, tile, D), lambda b, i: (b, i, 0)),
        out_shape=jax.ShapeDtypeStruct(x.shape, F32),
        scratch_shapes=[pltpu.VMEM((tile + 2 * POOL_HALO + POOL_PAD, D), F32),
                        pltpu.VMEM((tile + 2 * POOL_HALO, GROUP), F32),
                        pltpu.VMEM((tile + 2 * POOL_HALO, GROUP), F32)],
        compiler_params=_params(),
        name="pool_mixer",
    )(x, x, x, g, w, scale)


def _memfold_kernel(m_ref, g_ref, wkv_ref, wq_ref, wo_ref, mt_ref, vw_ref):
    mn = _rms(m_ref[0], g_ref[0]).astype(BF16)
    kv = _dot(mn, wkv_ref[0])
    k = kv[:, :D_MODEL].astype(BF16)
    v = kv[:, D_MODEL:].astype(BF16)
    for h in range(X_HEADS):
        cs = slice(h * X_HEAD_DIM, (h + 1) * X_HEAD_DIM)
        rs = slice(h * N_MEM, (h + 1) * N_MEM)
        mt = _dot_nt(k[:, cs], wq_ref[0, :, cs]) * (X_HEAD_DIM ** -0.5)
        mt_ref[0, 0, rs, :] = mt.astype(BF16)
        vw_ref[0, 0, rs, :] = _dot(v[:, cs], wo_ref[0, cs, :]).astype(BF16)


def _mem_fold(mem, g, w_kv, w_q, w_o):
    B = mem.shape[0]
    rows = X_HEADS * N_MEM
    out = jax.ShapeDtypeStruct((DEPTH, B, rows, D_MODEL), BF16)
    w_spec = pl.BlockSpec((1, D_MODEL, D_MODEL), lambda l, b: (l, 0, 0))
    o_spec = pl.BlockSpec((1, 1, rows, D_MODEL), lambda l, b: (l, b, 0, 0))
    return pl.pallas_call(
        _memfold_kernel,
        grid=(DEPTH, B),
        in_specs=[
            pl.BlockSpec((1, N_MEM, D_MODEL), lambda l, b: (b, 0, 0)),
            pl.BlockSpec((1, 1, D_MODEL), lambda l, b: (l, 0, 0)),
            pl.BlockSpec((1, D_MODEL, 2 * D_MODEL), lambda l, b: (l, 0, 0)),
            w_spec, w_spec,
        ],
        out_specs=[o_spec, o_spec],
        out_shape=[out, out],
        compiler_params=_params(),
        name="mem_fold",
    )(mem, g, w_kv, w_q, w_o)


def _xattn_kernel(x_ref, g_ref, mt_ref, vw_ref, o_ref):
    x = x_ref[...]
    hn = _rms(x, g_ref[...]).astype(BF16)
    st = _dot_nt(mt_ref[0, 0], hn)
    probs = []
    for h in range(X_HEADS):
        s = st[h * N_MEM:(h + 1) * N_MEM]
        m = jnp.max(s, axis=0, keepdims=True)
        e = jnp.exp(s - m)
        r = 1.0 / jnp.sum(e, axis=0, keepdims=True)
        probs.append((e * r).astype(BF16))
    pt = jnp.concatenate(probs, axis=0)
    o_ref[...] = x + lax.dot_general(pt, vw_ref[0, 0], (((0,), (0,)), ((), ())), preferred_element_type=F32)


def _xattn_layer(x2d, seq_len, g, mt, vw, layer):
    N, D = x2d.shape
    tile = TOKEN_TILE
    tps = seq_len // tile
    fold_spec = pl.BlockSpec((1, 1, X_HEADS * N_MEM, D), lambda i: (layer, i // tps, 0, 0))
    return pl.pallas_call(
        _xattn_kernel,
        grid=(N // tile,),
        in_specs=[
            pl.BlockSpec((tile, D), lambda i: (i, 0)),
            _resident((1, D)),
            fold_spec,
            fold_spec,
        ],
        out_specs=pl.BlockSpec((tile, D), lambda i: (i, 0)),
        out_shape=jax.ShapeDtypeStruct((N, D), F32),
        compiler_params=_params(),
        name="mem_xattn",
    )(x2d, g, mt, vw)


def _mlp_kernel(x_ref, g_ref, wu_ref, wd_ref, gf_ref, o_ref, *, final_norm):
    x = x_ref[...]
    hn = _rms(x, g_ref[...]).astype(BF16)
    acc = x
    for c in range(D_FF // FF_CHUNK):
        cs = slice(c * FF_CHUNK, (c + 1) * FF_CHUNK)
        u = jnp.maximum(_dot(hn, wu_ref[:, cs]), 0.0)
        acc = acc + _dot((u * u).astype(BF16), wd_ref[cs, :])
    if final_norm:
        acc = _rms(acc, gf_ref[...])
    o_ref[...] = acc


def _mlp_layer(x2d, g, w_up, w_down, g_final, final_norm):
    N, D = x2d.shape
    tile = TOKEN_TILE
    kern = functools.partial(_mlp_kernel, final_norm=final_norm)
    return pl.pallas_call(
        kern,
        grid=(N // tile,),
        in_specs=[
            pl.BlockSpec((tile, D), lambda i: (i, 0)),
            _resident((1, D)),
            _resident((D, D_FF)),
            _resident((D_FF, D)),
            _resident((1, D)),
        ],
        out_specs=pl.BlockSpec((tile, D), lambda i: (i, 0)),
        out_shape=jax.ShapeDtypeStruct((N, D), F32),
        compiler_params=_params(),
        name="sq_relu_mlp",
    )(x2d, g, w_up, w_down, g_final)


def _head_perm():
    idx = []
    for j in range(GQA_GROUP):
        for kh in range(N_KV):
            h = GQA_GROUP * kh + j
            idx.extend(range(h * HEAD_DIM, (h + 1) * HEAD_DIM))
    return jnp.asarray(idx, dtype=jnp.int32)


def _rope_tables(seq_len):
    inv_freq = ROPE_THETA ** (-jnp.arange(0, HEAD_DIM, 2, dtype=F32) / HEAD_DIM)
    ang = jnp.arange(seq_len, dtype=F32)[:, None] * inv_freq[None, :]
    cos, sin = jnp.cos(ang), jnp.sin(ang)
    zero = jnp.zeros_like(sin)
    reps = LANES // HEAD_DIM
    cos_t = jnp.tile(jnp.concatenate([cos, cos], axis=-1), (1, reps))
    sa_t = jnp.tile(jnp.concatenate([-sin, zero], axis=-1), (1, reps))
    sb_t = jnp.tile(jnp.concatenate([zero, sin], axis=-1), (1, reps))
    return cos_t, sa_t, sb_t, cos.T, sin.T


def _qkv_kernel(x_ref, g_ref, wqvt_ref, wk_ref, cos_ref, sa_ref, sb_ref, cost_ref, sint_ref,
                qt_ref, k_ref, vt_ref):
    hn = _rms(x_ref[...], g_ref[...]).astype(BF16)
    half = HEAD_DIM // 2
    nq = N_HEADS * HEAD_DIM

    qvt = _dot_nt(wqvt_ref[...], hn)
    cost, sint = cost_ref[...], sint_ref[...]
    scale = HEAD_DIM ** -0.5
    for h in range(N_HEADS):
        a = qvt[h * HEAD_DIM: h * HEAD_DIM + half]
        b = qvt[h * HEAD_DIM + half: (h + 1) * HEAD_DIM]
        qt_ref[0, h * HEAD_DIM: h * HEAD_DIM + half, :] = ((a * cost - b * sint) * scale).astype(BF16)
        qt_ref[0, h * HEAD_DIM + half: (h + 1) * HEAD_DIM, :] = ((b * cost + a * sint) * scale).astype(BF16)
    vt_ref[0] = qvt[nq:].astype(BF16)

    k = _dot(hn, wk_ref[...])
    cos, sa, sb = cos_ref[...], sa_ref[...], sb_ref[...]
    for cb in range(KV_WIDTH // LANES):
        cs = slice(cb * LANES, (cb + 1) * LANES)
        xb = k[:, cs]
        r = xb * cos + pltpu.roll(xb, LANES - half, 1) * sa + pltpu.roll(xb, half, 1) * sb
        k_ref[:, cs] = r.astype(BF16)


def _qkv_layer(x2d, seq_len, g, wqvt, wk, tables):
    N, D = x2d.shape
    tile = TOKEN_TILE
    tps = seq_len // tile
    B = N // seq_len
    nq = N_HEADS * HEAD_DIM
    cos_t, sa_t, sb_t, cost, sint = tables
    tab_spec = pl.BlockSpec((tile, LANES), lambda i: (i % tps, 0))
    tabt_spec = pl.BlockSpec((HEAD_DIM // 2, tile), lambda i: (0, i % tps))
    return pl.pallas_call(
        _qkv_kernel,
        grid=(N // tile,),
        in_specs=[
            pl.BlockSpec((tile, D), lambda i: (i, 0)),
            _resident((1, D)),
            _resident(wqvt.shape),
            _resident(wk.shape),
            tab_spec, tab_spec, tab_spec,
            tabt_spec, tabt_spec,
        ],
        out_specs=[
            pl.BlockSpec((1, nq, tile), lambda i: (i // tps, 0, i % tps)),
            pl.BlockSpec((tile, KV_WIDTH), lambda i: (i, 0)),
            pl.BlockSpec((1, KV_WIDTH, tile), lambda i: (i // tps, 0, i % tps)),
        ],
        out_shape=[
            jax.ShapeDtypeStruct((B, nq, seq_len), BF16),
            jax.ShapeDtypeStruct((N, KV_WIDTH), BF16),
            jax.ShapeDtypeStruct((B, KV_WIDTH, seq_len), BF16),
        ],
        compiler_params=_params(),
        name="qkv_rope",
    )(x2d, g, wqvt, wk, cos_t, sa_t, sb_t, cost, sint)


def _attend_block(qt, k, vt, sink_ref, prev_valid, next_valid):
    slab = KV_WIDTH
    qst = jnp.concatenate([qt[j * slab:(j + 1) * slab] for j in range(GQA_GROUP)], axis=1)
    lane_head = lax.shift_right_logical(
        lax.broadcasted_iota(jnp.int32, k.shape, 1), HEAD_DIM.bit_length() - 1)
    kbd = jnp.concatenate([jnp.where(lane_head == kh, k, jnp.zeros_like(k)) for kh in range(N_KV)], axis=0)
    s = _dot(kbd, qst)

    lanes = GQA_GROUP * Q_BLOCK
    r = lax.broadcasted_iota(jnp.int32, (WINDOW, lanes), 0)
    c = lax.broadcasted_iota(jnp.int32, (WINDOW, lanes), 1) & (Q_BLOCK - 1)
    prev_bias = jnp.where((r >= c) & prev_valid, 0.0, -1e30)
    next_bias = jnp.where((r <= c) & next_valid, 0.0, -1e30)

    probs, scales = [], []
    for kh in range(N_KV):
        base = kh * N_KEYS
        sk = jnp.concatenate([
            s[base:base + WINDOW] + prev_bias,
            s[base + WINDOW:base + WINDOW + Q_BLOCK],
            s[base + WINDOW + Q_BLOCK:base + N_KEYS] + next_bias], axis=0)
        sink = jnp.concatenate(
            [jnp.full((1, Q_BLOCK), sink_ref[GQA_GROUP * kh + j], F32) for j in range(GQA_GROUP)], axis=1)
        m = jnp.maximum(jnp.max(sk, axis=0, keepdims=True), sink)
        e = jnp.exp(sk - m)
        denom = jnp.sum(e, axis=0, keepdims=True) + jnp.exp(sink - m)
        probs.append(e.astype(BF16))
        scales.append(1.0 / denom)
    pt = jnp.concatenate(probs, axis=0)

    zero_v = jnp.zeros((HEAD_DIM, N_KEYS), BF16)
    vbd = jnp.concatenate(
        [jnp.concatenate([vt[kh * HEAD_DIM:(kh + 1) * HEAD_DIM] if rr == kh else zero_v
                          for rr in range(N_KV)], axis=0)
         for kh in range(N_KV)], axis=1)
    ot = _dot(vbd, pt)
    ot = jnp.concatenate([ot[kh * HEAD_DIM:(kh + 1) * HEAD_DIM] * scales[kh] for kh in range(N_KV)], axis=0)
    return jnp.concatenate([ot[:, j * Q_BLOCK:(j + 1) * Q_BLOCK] for j in range(GQA_GROUP)], axis=0)


def _winattn_kernel(sink_ref, x_ref, qt_ref, kp_ref, kc_ref, kn_ref, vp_ref, vc_ref, vn_ref, wo_ref, o_ref,
                    *, blocks):
    i = pl.program_id(1)
    last = pl.num_programs(1) - 1
    qt = qt_ref[0]
    k_all = jnp.concatenate([kp_ref[0], kc_ref[0], kn_ref[0]], axis=0)
    vt_all = jnp.concatenate([vp_ref[0], vc_ref[0], vn_ref[0]], axis=1)
    outs = []
    for blk in range(blocks):
        lo = blk * Q_BLOCK
        prev_valid = (i > 0) if blk == 0 else True
        next_valid = (i < last) if blk == blocks - 1 else True
        outs.append(_attend_block(qt[:, lo:lo + Q_BLOCK], k_all[lo:lo + N_KEYS], vt_all[:, lo:lo + N_KEYS],
                                  sink_ref, prev_valid, next_valid))
    at = jnp.concatenate(outs, axis=1).astype(BF16)
    proj = lax.dot_general(at, wo_ref[...], (((0,), (0,)), ((), ())), preferred_element_type=F32)
    o_ref[0] = x_ref[0] + proj


def _winattn_layer(x, qt, k, vt, sinks, wo):
    B, S, D = x.shape
    nq = qt.shape[1]
    blocks = ATTN_BLOCKS
    rows = blocks * Q_BLOCK
    ns = S // rows
    nb = S // Q_BLOCK
    x_spec = pl.BlockSpec((1, rows, D), lambda b, i: (b, i, 0))
    qt_spec = pl.BlockSpec((1, nq, rows), lambda b, i: (b, 0, i))
    prev = lambda i: jnp.maximum(i * blocks - 1, 0)
    nxt = lambda i: jnp.minimum((i + 1) * blocks, nb - 1)
    kern = functools.partial(_winattn_kernel, blocks=blocks)
    return pl.pallas_call(
        kern,
        grid=(B, ns),
        in_specs=[
            pl.BlockSpec(memory_space=pltpu.SMEM),
            x_spec, qt_spec,
            pl.BlockSpec((1, Q_BLOCK, KV_WIDTH), lambda b, i: (b, prev(i), 0)),
            pl.BlockSpec((1, rows, KV_WIDTH), lambda b, i: (b, i, 0)),
            pl.BlockSpec((1, Q_BLOCK, KV_WIDTH), lambda b, i: (b, nxt(i), 0)),
            pl.BlockSpec((1, KV_WIDTH, Q_BLOCK), lambda b, i: (b, 0, prev(i))),
            pl.BlockSpec((1, KV_WIDTH, rows), lambda b, i: (b, 0, i)),
            pl.BlockSpec((1, KV_WIDTH, Q_BLOCK), lambda b, i: (b, 0, nxt(i))),
            _resident(wo.shape),
        ],
        out_specs=x_spec,
        out_shape=jax.ShapeDtypeStruct((B, S, D), F32),
        compiler_params=_params(),
        name="window_attn",
    )(sinks, x, qt, k, k, k, vt, vt, vt, wo)


def _trunk(x, mem, p):
    B, S, D = x.shape
    N = B * S
    mt_mem, vw_mem = _mem_fold(mem, p["norm_mem"], p["x_wkv"], p["x_wq"], p["x_wo"])
    tables = _rope_tables(S)

    def row(a, i):
        return a[i].reshape(1, -1)

    x = _pool_layer(x, row(p["norm_mix"], 0), p["pool_w"][0], row(p["pool_scale"], 0))
    x2 = x.reshape(N, D)
    for i in range(DEPTH):
        if i % 2 == 1:
            j = i // 2
            qt, k, vt = _qkv_layer(x2, S, row(p["norm_mix"], i), p["attn_wqvt"][j], p["attn_wk"][j], tables)
            x2 = _winattn_layer(x2.reshape(B, S, D), qt, k.reshape(B, S, -1), vt,
                                p["attn_sink"][j], p["attn_o"][j]).reshape(N, D)
        x2 = _xattn_layer(x2, S, row(p["norm_x"], i), mt_mem, vw_mem, i)
        x2 = _mlp_layer(x2, row(p["norm_mlp"], i), p["w_up"][i], p["w_down"][i],
                        p["norm_final"].reshape(1, -1), final_norm=(i == DEPTH - 1))
    return x2.reshape(B, S, D)


def kernel(x_prompt, x_sample, mem_prompt, mem_sample, norm_mix, pool_w, pool_scale, attn_qkv, attn_o,
           attn_sink, norm_x, norm_mem, x_wq, x_wkv, x_wo, norm_mlp, w_up, w_down, norm_final):
    nq, nk = N_HEADS * HEAD_DIM, N_KV * HEAD_DIM
    perm = _head_perm()
    p = dict(
        norm_mix=norm_mix, pool_scale=pool_scale, attn_sink=attn_sink, norm_x=norm_x,
        norm_mem=norm_mem.reshape(DEPTH, 1, D_MODEL), norm_mlp=norm_mlp, norm_final=norm_final,
        pool_w=pool_w.astype(BF16),
        attn_wqvt=jnp.swapaxes(
            jnp.concatenate([attn_qkv[:, :, :nq][:, :, perm], attn_qkv[:, :, nq + nk:]], axis=2), 1, 2).astype(BF16),
        attn_wk=attn_qkv[:, :, nq:nq + nk].astype(BF16),
        attn_o=attn_o[:, perm, :].astype(BF16),
        x_wq=x_wq.astype(BF16), x_wkv=x_wkv.astype(BF16), x_wo=x_wo.astype(BF16),
        w_up=w_up.astype(BF16), w_down=w_down.astype(BF16),
    )
    return (_trunk(x_prompt, mem_prompt, p), _trunk(x_sample, mem_sample, p))
```

```python
import functools

import jax
import jax.numpy as jnp
from jax import lax
from jax.experimental import pallas as pl
from jax.experimental.pallas import tpu as pltpu

D_MODEL = 1024
DEPTH = 2
EPS = 1e-6
POOL_WINDOWS = (2, 4, 8, 16)
GROUP = D_MODEL // len(POOL_WINDOWS)
HEAD_DIM = 64
N_HEADS = 16
N_KV = 4
GQA_GROUP = N_HEADS // N_KV
WINDOW = 128
ROPE_THETA = 10000.0
N_MEM = 256
X_HEADS = 4
X_HEAD_DIM = D_MODEL // X_HEADS
D_FF = 4 * D_MODEL

LANES = 128
SUBLANES = 8
POOL_HALO = 16
POOL_PAD = SUBLANES
VMEM_LIMIT = 56 * 1024 * 1024

TOKEN_TILE = 512
FF_CHUNK = 1024
Q_BLOCK = 128
ATTN_BLOCKS = 4
N_KEYS = Q_BLOCK + 2 * WINDOW
KV_WIDTH = N_KV * HEAD_DIM

BF16 = jnp.bfloat16
F32 = jnp.float32


def _params():
    return pltpu.CompilerParams(vmem_limit_bytes=VMEM_LIMIT)


def _resident(shape):
    zeros = (0,) * len(shape)
    return pl.BlockSpec(shape, lambda *_: zeros, pipeline_mode=pl.Buffered(1))


def _rms(x, g):
    ms = jnp.mean(x * x, axis=-1, keepdims=True)
    return (x * lax.rsqrt(ms + EPS)) * g


def _dot(a, b):
    return jnp.dot(a, b, preferred_element_type=F32)


def _dot_nt(a, b):
    return lax.dot_general(a, b, (((1,), (1,)), ((), ())), preferred_element_type=F32)


def _pool_kernel(xc_ref, xp_ref, xn_ref, g_ref, w_ref, sc_ref, o_ref, hbuf, s1, s2, *, seq_len, tile):
    i = pl.program_id(1)
    last = pl.num_programs(1) - 1
    g = g_ref[...]
    x = xc_ref[0]
    hn = _rms(x, g)
    hp = _rms(xp_ref[0], g) * jnp.where(i > 0, 1.0, 0.0)
    hx = _rms(xn_ref[0], g) * jnp.where(i < last, 1.0, 0.0)
    H = POOL_HALO
    hbuf[0:H, :] = hp
    hbuf[H:H + tile, :] = hn
    hbuf[H + tile:H + tile + H, :] = hx
    hbuf[H + tile + H:, :] = jnp.zeros((POOL_PAD, hbuf.shape[1]), F32)

    n8 = tile + H
    n4 = n8 + POOL_PAD
    n2 = n4 + POOL_PAD
    t = i * tile + lax.broadcasted_iota(jnp.int32, (tile, 1), 0)
    for gi, w in enumerate(POOL_WINDOWS):
        cs = slice(gi * GROUP, (gi + 1) * GROUP)
        if w == 2:
            total = hbuf[pl.ds(H - 1, tile), cs] + hbuf[pl.ds(H, tile), cs]
        else:
            s1[0:n2, :] = hbuf[0:n2, cs] + hbuf[1:n2 + 1, cs]
            if w == 4:
                total = s1[pl.ds(H - 2, tile), :] + s1[pl.ds(H, tile), :]
            else:
                s2[0:n4, :] = s1[0:n4, :] + s1[2:n4 + 2, :]
                if w == 8:
                    total = s2[pl.ds(H - 4, tile), :] + s2[pl.ds(H, tile), :]
                else:
                    s1[0:n8, :] = s2[0:n8, :] + s2[4:n8 + 4, :]
                    total = s1[pl.ds(H - 8, tile), :] + s1[pl.ds(H, tile), :]
        lo = jnp.maximum(t - w // 2, 0)
        hi = jnp.minimum(t + w // 2, seq_len)
        count = (hi - lo).astype(F32)
        diff = total / count - hn[:, cs]
        y = _dot(diff.astype(BF16), w_ref[gi])
        o_ref[0, :, cs] = x[:, cs] + y * sc_ref[:, cs]


def _pool_layer(x, g, w, scale):
    B, S, D = x.shape
    tile = TOKEN_TILE
    nt = S // tile
    hb = tile // POOL_HALO
    nhb = S // POOL_HALO
    kern = functools.partial(_pool_kernel, seq_len=S, tile=tile)
    return pl.pallas_call(
        kern,
        grid=(B, nt),
        in_specs=[
            pl.BlockSpec((1, tile, D), lambda b, i: (b, i, 0)),
            pl.BlockSpec((1, POOL_HALO, D), lambda b, i: (b, jnp.maximum(i * hb - 1, 0), 0)),
            pl.BlockSpec((1, POOL_HALO, D), lambda b, i: (b, jnp.minimum((i + 1) * hb, nhb - 1), 0)),
            _resident((1, D)),
            _resident(w.shape),
            _resident((1, D)),
        ],
        out_specs=pl.BlockSpec((1, tile, D), lambda b, i: (b, i, 0)),
        out_shape=jax.ShapeDtypeStruct(x.shape, F32),
        scratch_shapes=[pltpu.VMEM((tile + 2 * POOL_HALO + POOL_PAD, D), F32),
                        pltpu.VMEM((tile + 2 * POOL_HALO, GROUP), F32),
                        pltpu.VMEM((tile + 2 * POOL_HALO, GROUP), F32)],
        compiler_params=_params(),
        name="pool_mixer",
    )(x, x, x, g, w, scale)


def _memfold_kernel(m_ref, g_ref, wkv_ref, wq_ref, wo_ref, mt_ref, vw_ref):
    mn = _rms(m_ref[0], g_ref[0]).astype(BF16)
    kv = _dot(mn, wkv_ref[0])
    k = kv[:, :D_MODEL].astype(BF16)
    v = kv[:, D_MODEL:].astype(BF16)
    for h in range(X_HEADS):
        cs = slice(h * X_HEAD_DIM, (h + 1) * X_HEAD_DIM)
        rs = slice(h * N_MEM, (h + 1) * N_MEM)
        mt = _dot_nt(k[:, cs], wq_ref[0, :, cs]) * (X_HEAD_DIM ** -0.5)
        mt_ref[0, 0, rs, :] = mt.astype(BF16)
        vw_ref[0, 0, rs, :] = _dot(v[:, cs], wo_ref[0, cs, :]).astype(BF16)


def _mem_fold(mem, g, w_kv, w_q, w_o):
    B = mem.shape[0]
    rows = X_HEADS * N_MEM
    out = jax.ShapeDtypeStruct((DEPTH, B, rows, D_MODEL), BF16)
    w_spec = pl.BlockSpec((1, D_MODEL, D_MODEL), lambda l, b: (l, 0, 0))
    o_spec = pl.BlockSpec((1, 1, rows, D_MODEL), lambda l, b: (l, b, 0, 0))
    return pl.pallas_call(
        _memfold_kernel,
        grid=(DEPTH, B),
        in_specs=[
            pl.BlockSpec((1, N_MEM, D_MODEL), lambda l, b: (b, 0, 0)),
            pl.BlockSpec((1, 1, D_MODEL), lambda l, b: (l, 0, 0)),
            pl.BlockSpec((1, D_MODEL, 2 * D_MODEL), lambda l, b: (l, 0, 0)),
            w_spec, w_spec,
        ],
        out_specs=[o_spec, o_spec],
        out_shape=[out, out],
        compiler_params=_params(),
        name="mem_fold",
    )(mem, g, w_kv, w_q, w_o)


def _xattn_mlp_kernel(x_ref, gx_ref, mt_ref, vw_ref, gm_ref, wu_ref, wd_ref, gf_ref, o_ref, *, final_norm):
    x = x_ref[...]
    hn = _rms(x, gx_ref[...]).astype(BF16)
    st = _dot_nt(mt_ref[0, 0], hn)
    probs = []
    for h in range(X_HEADS):
        s = st[h * N_MEM:(h + 1) * N_MEM]
        m = jnp.max(s, axis=0, keepdims=True)
        e = jnp.exp(s - m)
        r = 1.0 / jnp.sum(e, axis=0, keepdims=True)
        probs.append((e * r).astype(BF16))
    pt = jnp.concatenate(probs, axis=0)
    x = x + lax.dot_general(pt, vw_ref[0, 0], (((0,), (0,)), ((), ())), preferred_element_type=F32)

    hn = _rms(x, gm_ref[...]).astype(BF16)
    acc = x
    for c in range(D_FF // FF_CHUNK):
        cs = slice(c * FF_CHUNK, (c + 1) * FF_CHUNK)
        u = jnp.maximum(_dot(hn, wu_ref[:, cs]), 0.0)
        acc = acc + _dot((u * u).astype(BF16), wd_ref[cs, :])
    if final_norm:
        acc = _rms(acc, gf_ref[...])
    o_ref[...] = acc


def _xattn_mlp_layer(x2d, seq_len, layer, gx, mt, vw, gm, w_up, w_down, g_final, final_norm):
    N, D = x2d.shape
    tile = TOKEN_TILE
    tps = seq_len // tile
    fold_spec = pl.BlockSpec((1, 1, X_HEADS * N_MEM, D), lambda i: (layer, i // tps, 0, 0))
    kern = functools.partial(_xattn_mlp_kernel, final_norm=final_norm)
    return pl.pallas_call(
        kern,
        grid=(N // tile,),
        in_specs=[
            pl.BlockSpec((tile, D), lambda i: (i, 0)),
            _resident((1, D)),
            fold_spec,
            fold_spec,
            _resident((1, D)),
            _resident((D, D_FF)),
            _resident((D_FF, D)),
            _resident((1, D)),
        ],
        out_specs=pl.BlockSpec((tile, D), lambda i: (i, 0)),
        out_shape=jax.ShapeDtypeStruct((N, D), F32),
        compiler_params=_params(),
        name="xattn_mlp",
    )(x2d, gx, mt, vw, gm, w_up, w_down, g_final)


def _head_perm():
    idx = []
    for j in range(GQA_GROUP):
        for kh in range(N_KV):
            h = GQA_GROUP * kh + j
            idx.extend(range(h * HEAD_DIM, (h + 1) * HEAD_DIM))
    return jnp.asarray(idx, dtype=jnp.int32)


def _rope_tables(seq_len):
    inv_freq = ROPE_THETA ** (-jnp.arange(0, HEAD_DIM, 2, dtype=F32) / HEAD_DIM)
    ang = jnp.arange(seq_len, dtype=F32)[:, None] * inv_freq[None, :]
    cos, sin = jnp.cos(ang), jnp.sin(ang)
    zero = jnp.zeros_like(sin)
    reps = LANES // HEAD_DIM
    cos_t = jnp.tile(jnp.concatenate([cos, cos], axis=-1), (1, reps))
    sa_t = jnp.tile(jnp.concatenate([-sin, zero], axis=-1), (1, reps))
    sb_t = jnp.tile(jnp.concatenate([zero, sin], axis=-1), (1, reps))
    return cos_t, sa_t, sb_t, cos.T, sin.T


def _qkv_kernel(x_ref, g_ref, wqvt_ref, wk_ref, cos_ref, sa_ref, sb_ref, cost_ref, sint_ref,
                qt_ref, k_ref, vt_ref):
    hn = _rms(x_ref[...], g_ref[...]).astype(BF16)
    half = HEAD_DIM // 2
    nq = N_HEADS * HEAD_DIM

    qvt = _dot_nt(wqvt_ref[...], hn)
    cost, sint = cost_ref[...], sint_ref[...]
    scale = HEAD_DIM ** -0.5
    for h in range(N_HEADS):
        a = qvt[h * HEAD_DIM: h * HEAD_DIM + half]
        b = qvt[h * HEAD_DIM + half: (h + 1) * HEAD_DIM]
        qt_ref[0, h * HEAD_DIM: h * HEAD_DIM + half, :] = ((a * cost - b * sint) * scale).astype(BF16)
        qt_ref[0, h * HEAD_DIM + half: (h + 1) * HEAD_DIM, :] = ((b * cost + a * sint) * scale).astype(BF16)
    vt_ref[0] = qvt[nq:].astype(BF16)

    k = _dot(hn, wk_ref[...])
    cos, sa, sb = cos_ref[...], sa_ref[...], sb_ref[...]
    for cb in range(KV_WIDTH // LANES):
        cs = slice(cb * LANES, (cb + 1) * LANES)
        xb = k[:, cs]
        r = xb * cos + pltpu.roll(xb, LANES - half, 1) * sa + pltpu.roll(xb, half, 1) * sb
        k_ref[:, cs] = r.astype(BF16)


def _qkv_layer(x2d, seq_len, g, wqvt, wk, tables):
    N, D = x2d.shape
    tile = TOKEN_TILE
    tps = seq_len // tile
    B = N // seq_len
    nq = N_HEADS * HEAD_DIM
    cos_t, sa_t, sb_t, cost, sint = tables
    tab_spec = pl.BlockSpec((tile, LANES), lambda i: (i % tps, 0))
    tabt_spec = pl.BlockSpec((HEAD_DIM // 2, tile), lambda i: (0, i % tps))
    return pl.pallas_call(
        _qkv_kernel,
        grid=(N // tile,),
        in_specs=[
            pl.BlockSpec((tile, D), lambda i: (i, 0)),
            _resident((1, D)),
            _resident(wqvt.shape),
            _resident(wk.shape),
            tab_spec, tab_spec, tab_spec,
            tabt_spec, tabt_spec,
        ],
        out_specs=[
            pl.BlockSpec((1, nq, tile), lambda i: (i // tps, 0, i % tps)),
            pl.BlockSpec((tile, KV_WIDTH), lambda i: (i, 0)),
            pl.BlockSpec((1, KV_WIDTH, tile), lambda i: (i // tps, 0, i % tps)),
        ],
        out_shape=[
            jax.ShapeDtypeStruct((B, nq, seq_len), BF16),
            jax.ShapeDtypeStruct((N, KV_WIDTH), BF16),
            jax.ShapeDtypeStruct((B, KV_WIDTH, seq_len), BF16),
        ],
        compiler_params=_params(),
        name="qkv_rope",
    )(x2d, g, wqvt, wk, cos_t, sa_t, sb_t, cost, sint)


def _attend_block(qt, k, vt, sink_ref, prev_valid, next_valid):
    slab = KV_WIDTH
    qst = jnp.concatenate([qt[j * slab:(j + 1) * slab] for j in range(GQA_GROUP)], axis=1)
    lane_head = lax.shift_right_logical(
        lax.broadcasted_iota(jnp.int32, k.shape, 1), HEAD_DIM.bit_length() - 1)
    kbd = jnp.concatenate([jnp.where(lane_head == kh, k, jnp.zeros_like(k)) for kh in range(N_KV)], axis=0)
    s = _dot(kbd, qst)

    lanes = GQA_GROUP * Q_BLOCK
    r = lax.broadcasted_iota(jnp.int32, (WINDOW, lanes), 0)
    c = lax.broadcasted_iota(jnp.int32, (WINDOW, lanes), 1) & (Q_BLOCK - 1)
    prev_bias = jnp.where((r >= c) & prev_valid, 0.0, -1e30)
    next_bias = jnp.where((r <= c) & next_valid, 0.0, -1e30)

    probs, scales = [], []
    for kh in range(N_KV):
        base = kh * N_KEYS
        sk = jnp.concatenate([
            s[base:base + WINDOW] + prev_bias,
            s[base + WINDOW:base + WINDOW + Q_BLOCK],
            s[base + WINDOW + Q_BLOCK:base + N_KEYS] + next_bias], axis=0)
        sink = jnp.concatenate(
            [jnp.full((1, Q_BLOCK), sink_ref[GQA_GROUP * kh + j], F32) for j in range(GQA_GROUP)], axis=1)
        m = jnp.maximum(jnp.max(sk, axis=0, keepdims=True), sink)
        e = jnp.exp(sk - m)
        denom = jnp.sum(e, axis=0, keepdims=True) + jnp.exp(sink - m)
        probs.append(e.astype(BF16))
        scales.append(1.0 / denom)
    pt = jnp.concatenate(probs, axis=0)

    zero_v = jnp.zeros((HEAD_DIM, N_KEYS), BF16)
    vbd = jnp.concatenate(
        [jnp.concatenate([vt[kh * HEAD_DIM:(kh + 1) * HEAD_DIM] if rr == kh else zero_v
                          for rr in range(N_KV)], axis=0)
         for kh in range(N_KV)], axis=1)
    ot = _dot(vbd, pt)
    ot = jnp.concatenate([ot[kh * HEAD_DIM:(kh + 1) * HEAD_DIM] * scales[kh] for kh in range(N_KV)], axis=0)
    return jnp.concatenate([ot[:, j * Q_BLOCK:(j + 1) * Q_BLOCK] for j in range(GQA_GROUP)], axis=0)


def _winattn_kernel(sink_ref, x_ref, qt_ref, kp_ref, kc_ref, kn_ref, vp_ref, vc_ref, vn_ref, wo_ref, o_ref,
                    *, blocks):
    i = pl.program_id(1)
    last = pl.num_programs(1) - 1
    qt = qt_ref[0]
    k_all = jnp.concatenate([kp_ref[0], kc_ref[0], kn_ref[0]], axis=0)
    vt_all = jnp.concatenate([vp_ref[0], vc_ref[0], vn_ref[0]], axis=1)
    outs = []
    for blk in range(blocks):
        lo = blk * Q_BLOCK
        prev_valid = (i > 0) if blk == 0 else True
        next_valid = (i < last) if blk == blocks - 1 else True
        outs.append(_attend_block(qt[:, lo:lo + Q_BLOCK], k_all[lo:lo + N_KEYS], vt_all[:, lo:lo + N_KEYS],
                                  sink_ref, prev_valid, next_valid))
    at = jnp.concatenate(outs, axis=1).astype(BF16)
    proj = lax.dot_general(at, wo_ref[...], (((0,), (0,)), ((), ())), preferred_element_type=F32)
    o_ref[0] = x_ref[0] + proj


def _winattn_layer(x, qt, k, vt, sinks, wo):
    B, S, D = x.shape
    nq = qt.shape[1]
    blocks = ATTN_BLOCKS
    rows = blocks * Q_BLOCK
    ns = S // rows
    nb = S // Q_BLOCK
    x_spec = pl.BlockSpec((1, rows, D), lambda b, i: (b, i, 0))
    qt_spec = pl.BlockSpec((1, nq, rows), lambda b, i: (b, 0, i))
    prev = lambda i: jnp.maximum(i * blocks - 1, 0)
    nxt = lambda i: jnp.minimum((i + 1) * blocks, nb - 1)
    kern = functools.partial(_winattn_kernel, blocks=blocks)
    return pl.pallas_call(
        kern,
        grid=(B, ns),
        in_specs=[
            pl.BlockSpec(memory_space=pltpu.SMEM),
            x_spec, qt_spec,
            pl.BlockSpec((1, Q_BLOCK, KV_WIDTH), lambda b, i: (b, prev(i), 0)),
            pl.BlockSpec((1, rows, KV_WIDTH), lambda b, i: (b, i, 0)),
            pl.BlockSpec((1, Q_BLOCK, KV_WIDTH), lambda b, i: (b, nxt(i), 0)),
            pl.BlockSpec((1, KV_WIDTH, Q_BLOCK), lambda b, i: (b, 0, prev(i))),
            pl.BlockSpec((1, KV_WIDTH, rows), lambda b, i: (b, 0, i)),
            pl.BlockSpec((1, KV_WIDTH, Q_BLOCK), lambda b, i: (b, 0, nxt(i))),
            _resident(wo.shape),
        ],
        out_specs=x_spec,
        out_shape=jax.ShapeDtypeStruct((B, S, D), F32),
        compiler_params=_params(),
        name="window_attn",
    )(sinks, x, qt, k, k, k, vt, vt, vt, wo)


def _trunk(x, mem, p):
    B, S, D = x.shape
    N = B * S
    mt_mem, vw_mem = _mem_fold(mem, p["norm_mem"], p["x_wkv"], p["x_wq"], p["x_wo"])
    tables = _rope_tables(S)

    def row(a, i):
        return a[i].reshape(1, -1)

    x = _pool_layer(x, row(p["norm_mix"], 0), p["pool_w"][0], row(p["pool_scale"], 0))
    x2 = x.reshape(N, D)
    for i in range(DEPTH):
        if i % 2 == 1:
            j = i // 2
            qt, k, vt = _qkv_layer(x2, S, row(p["norm_mix"], i), p["attn_wqvt"][j], p["attn_wk"][j], tables)
            x2 = _winattn_layer(x2.reshape(B, S, D), qt, k.reshape(B, S, -1), vt,
                                p["attn_sink"][j], p["attn_o"][j]).reshape(N, D)
        x2 = _xattn_mlp_layer(x2, S, i, row(p["norm_x"], i), mt_mem, vw_mem, row(p["norm_mlp"], i),
                              p["w_up"][i], p["w_down"][i], p["norm_final"].reshape(1, -1),
                              final_norm=(i == DEPTH - 1))
    return x2.reshape(B, S, D)


def kernel(x_prompt, x_sample, mem_prompt, mem_sample, norm_mix, pool_w, pool_scale, attn_qkv, attn_o,
           attn_sink, norm_x, norm_mem, x_wq, x_wkv, x_wo, norm_mlp, w_up, w_down, norm_final):
    nq, nk = N_HEADS * HEAD_DIM, N_KV * HEAD_DIM
    perm = _head_perm()
    p = dict(
        norm_mix=norm_mix, pool_scale=pool_scale, attn_sink=attn_sink, norm_x=norm_x,
        norm_mem=norm_mem.reshape(DEPTH, 1, D_MODEL), norm_mlp=norm_mlp, norm_final=norm_final,
        pool_w=pool_w.astype(BF16),
        attn_wqvt=jnp.swapaxes(
            jnp.concatenate([attn_qkv[:, :, :nq][:, :, perm], attn_qkv[:, :, nq + nk:]], axis=2), 1, 2).astype(BF16),
        attn_wk=attn_qkv[:, :, nq:nq + nk].astype(BF16),
        attn_o=attn_o[:, perm, :].astype(BF16),
        x_wq=x_wq.astype(BF16), x_wkv=x_wkv.astype(BF16), x_wo=x_wo.astype(BF16),
        w_up=w_up.astype(BF16), w_down=w_down.astype(BF16),
    )
    return (_trunk(x_prompt, mem_prompt, p), _trunk(x_sample, mem_sample, p))
```

```python
import functools

import jax
import jax.numpy as jnp
from jax import lax
from jax.experimental import pallas as pl
from jax.experimental.pallas import tpu as pltpu

D_MODEL = 1024
DEPTH = 2
EPS = 1e-6
POOL_WINDOWS = (2, 4, 8, 16)
GROUP = D_MODEL // len(POOL_WINDOWS)
HEAD_DIM = 64
N_HEADS = 16
N_KV = 4
GQA_GROUP = N_HEADS // N_KV
WINDOW = 128
ROPE_THETA = 10000.0
N_MEM = 256
X_HEADS = 4
X_HEAD_DIM = D_MODEL // X_HEADS
D_FF = 4 * D_MODEL

LANES = 128
SUBLANES = 8
POOL_HALO = 16
POOL_PAD = SUBLANES
VMEM_LIMIT = 56 * 1024 * 1024

TOKEN_TILE = 512
STREAM_TILE = 1024
FF_CHUNK = 1024
Q_BLOCK = 128
ATTN_BLOCKS = 8
N_KEYS = Q_BLOCK + 2 * WINDOW
KV_WIDTH = N_KV * HEAD_DIM

BF16 = jnp.bfloat16
F32 = jnp.float32


def _params():
    return pltpu.CompilerParams(vmem_limit_bytes=VMEM_LIMIT)


def _resident(shape):
    zeros = (0,) * len(shape)
    return pl.BlockSpec(shape, lambda *_: zeros, pipeline_mode=pl.Buffered(1))


def _rms(x, g):
    ms = jnp.mean(x * x, axis=-1, keepdims=True)
    return (x * lax.rsqrt(ms + EPS)) * g


def _dot(a, b):
    return jnp.dot(a, b, preferred_element_type=F32)


def _dot_nt(a, b):
    return lax.dot_general(a, b, (((1,), (1,)), ((), ())), preferred_element_type=F32)


def _pool_kernel(xc_ref, xp_ref, xn_ref, g_ref, w_ref, sc_ref, o_ref, hbuf, s1, s2, *, seq_len, tile):
    i = pl.program_id(1)
    last = pl.num_programs(1) - 1
    g = g_ref[...]
    x = xc_ref[0]
    hn = _rms(x, g)
    hp = _rms(xp_ref[0], g) * jnp.where(i > 0, 1.0, 0.0)
    hx = _rms(xn_ref[0], g) * jnp.where(i < last, 1.0, 0.0)
    H = POOL_HALO
    hbuf[0:H, :] = hp
    hbuf[H:H + tile, :] = hn
    hbuf[H + tile:H + tile + H, :] = hx
    hbuf[H + tile + H:, :] = jnp.zeros((POOL_PAD, hbuf.shape[1]), F32)

    n8 = tile + H
    n4 = n8 + POOL_PAD
    n2 = n4 + POOL_PAD
    t = i * tile + lax.broadcasted_iota(jnp.int32, (tile, 1), 0)
    left = jnp.minimum(t, H).astype(F32)
    right = jnp.minimum(seq_len - t, H).astype(F32)
    for gi, w in enumerate(POOL_WINDOWS):
        cs = slice(gi * GROUP, (gi + 1) * GROUP)
        if w == 2:
            total = hbuf[pl.ds(H - 1, tile), cs] + hbuf[pl.ds(H, tile), cs]
        else:
            s1[0:n2, :] = hbuf[0:n2, cs] + hbuf[1:n2 + 1, cs]
            if w == 4:
                total = s1[pl.ds(H - 2, tile), :] + s1[pl.ds(H, tile), :]
            else:
                s2[0:n4, :] = s1[0:n4, :] + s1[2:n4 + 2, :]
                if w == 8:
                    total = s2[pl.ds(H - 4, tile), :] + s2[pl.ds(H, tile), :]
                else:
                    s1[0:n8, :] = s2[0:n8, :] + s2[4:n8 + 4, :]
                    total = s1[pl.ds(H - 8, tile), :] + s1[pl.ds(H, tile), :]
        count = jnp.minimum(left, float(w // 2)) + jnp.minimum(right, float(w // 2))
        diff = total * (1.0 / count) - hn[:, cs]
        y = _dot(diff.astype(BF16), w_ref[gi])
        o_ref[0, :, cs] = x[:, cs] + y * sc_ref[:, cs]


def _pool_layer(x, g, w, scale):
    B, S, D = x.shape
    tile = STREAM_TILE
    nt = S // tile
    hb = tile // POOL_HALO
    nhb = S // POOL_HALO
    kern = functools.partial(_pool_kernel, seq_len=S, tile=tile)
    return pl.pallas_call(
        kern,
        grid=(B, nt),
        in_specs=[
            pl.BlockSpec((1, tile, D), lambda b, i: (b, i, 0)),
            pl.BlockSpec((1, POOL_HALO, D), lambda b, i: (b, jnp.maximum(i * hb - 1, 0), 0)),
            pl.BlockSpec((1, POOL_HALO, D), lambda b, i: (b, jnp.minimum((i + 1) * hb, nhb - 1), 0)),
            _resident((1, D)),
            _resident(w.shape),
            _resident((1, D)),
        ],
        out_specs=pl.BlockSpec((1, tile, D), lambda b, i: (b, i, 0)),
        out_shape=jax.ShapeDtypeStruct(x.shape, F32),
        scratch_shapes=[pltpu.VMEM((tile + 2 * POOL_HALO + POOL_PAD, D), F32),
                        pltpu.VMEM((tile + 2 * POOL_HALO, GROUP), F32),
                        pltpu.VMEM((tile + 2 * POOL_HALO, GROUP), F32)],
        compiler_params=_params(),
        name="pool_mixer",
    )(x, x, x, g, w, scale)


def _memfold_kernel(m_ref, g_ref, wkv_ref, wq_ref, wo_ref, mt_ref, vw_ref):
    mn = _rms(m_ref[0], g_ref[0]).astype(BF16)
    kv = _dot(mn, wkv_ref[0])
    k = kv[:, :D_MODEL].astype(BF16)
    v = kv[:, D_MODEL:].astype(BF16)
    for h in range(X_HEADS):
        cs = slice(h * X_HEAD_DIM, (h + 1) * X_HEAD_DIM)
        rs = slice(h * N_MEM, (h + 1) * N_MEM)
        mt = _dot_nt(k[:, cs], wq_ref[0, :, cs]) * (X_HEAD_DIM ** -0.5)
        mt_ref[0, 0, rs, :] = mt.astype(BF16)
        vw_ref[0, 0, rs, :] = _dot(v[:, cs], wo_ref[0, cs, :]).astype(BF16)


def _mem_fold(mem, g, w_kv, w_q, w_o):
    B = mem.shape[0]
    rows = X_HEADS * N_MEM
    out = jax.ShapeDtypeStruct((DEPTH, B, rows, D_MODEL), BF16)
    w_spec = pl.BlockSpec((1, D_MODEL, D_MODEL), lambda l, b: (l, 0, 0))
    o_spec = pl.BlockSpec((1, 1, rows, D_MODEL), lambda l, b: (l, b, 0, 0))
    return pl.pallas_call(
        _memfold_kernel,
        grid=(DEPTH, B),
        in_specs=[
            pl.BlockSpec((1, N_MEM, D_MODEL), lambda l, b: (b, 0, 0)),
            pl.BlockSpec((1, 1, D_MODEL), lambda l, b: (l, 0, 0)),
            pl.BlockSpec((1, D_MODEL, 2 * D_MODEL), lambda l, b: (l, 0, 0)),
            w_spec, w_spec,
        ],
        out_specs=[o_spec, o_spec],
        out_shape=[out, out],
        compiler_params=_params(),
        name="mem_fold",
    )(mem, g, w_kv, w_q, w_o)


def _xattn_mlp_kernel(x_ref, gx_ref, mt_ref, vw_ref, gm_ref, wu_ref, wd_ref, gf_ref, o_ref, *, final_norm):
    x = x_ref[...]
    hn = _rms(x, gx_ref[...]).astype(BF16)
    st = _dot_nt(mt_ref[0, 0], hn)
    probs = []
    for h in range(X_HEADS):
        s = st[h * N_MEM:(h + 1) * N_MEM]
        m = jnp.max(s, axis=0, keepdims=True)
        e = jnp.exp(s - m)
        r = 1.0 / jnp.sum(e, axis=0, keepdims=True)
        probs.append((e * r).astype(BF16))
    pt = jnp.concatenate(probs, axis=0)
    x = x + lax.dot_general(pt, vw_ref[0, 0], (((0,), (0,)), ((), ())), preferred_element_type=F32)

    hn = _rms(x, gm_ref[...]).astype(BF16)
    acc = x
    for c in range(D_FF // FF_CHUNK):
        cs = slice(c * FF_CHUNK, (c + 1) * FF_CHUNK)
        u = jnp.maximum(_dot(hn, wu_ref[:, cs]), 0.0)
        acc = acc + _dot((u * u).astype(BF16), wd_ref[cs, :])
    if final_norm:
        acc = _rms(acc, gf_ref[...])
    o_ref[...] = acc


def _xattn_mlp_layer(x2d, seq_len, layer, gx, mt, vw, gm, w_up, w_down, g_final, final_norm):
    N, D = x2d.shape
    tile = TOKEN_TILE
    tps = seq_len // tile
    fold_spec = pl.BlockSpec((1, 1, X_HEADS * N_MEM, D), lambda i: (layer, i // tps, 0, 0))
    kern = functools.partial(_xattn_mlp_kernel, final_norm=final_norm)
    return pl.pallas_call(
        kern,
        grid=(N // tile,),
        in_specs=[
            pl.BlockSpec((tile, D), lambda i: (i, 0)),
            _resident((1, D)),
            fold_spec,
            fold_spec,
            _resident((1, D)),
            _resident((D, D_FF)),
            _resident((D_FF, D)),
            _resident((1, D)),
        ],
        out_specs=pl.BlockSpec((tile, D), lambda i: (i, 0)),
        out_shape=jax.ShapeDtypeStruct((N, D), F32),
        compiler_params=_params(),
        name="xattn_mlp",
    )(x2d, gx, mt, vw, gm, w_up, w_down, g_final)


def _head_perm():
    idx = []
    for j in range(GQA_GROUP):
        for kh in range(N_KV):
            h = GQA_GROUP * kh + j
            idx.extend(range(h * HEAD_DIM, (h + 1) * HEAD_DIM))
    return jnp.asarray(idx, dtype=jnp.int32)


def _rope_tables(seq_len):
    inv_freq = ROPE_THETA ** (-jnp.arange(0, HEAD_DIM, 2, dtype=F32) / HEAD_DIM)
    ang = jnp.arange(seq_len, dtype=F32)[:, None] * inv_freq[None, :]
    cos, sin = jnp.cos(ang), jnp.sin(ang)
    zero = jnp.zeros_like(sin)
    reps = LANES // HEAD_DIM
    cos_t = jnp.tile(jnp.concatenate([cos, cos], axis=-1), (1, reps))
    sa_t = jnp.tile(jnp.concatenate([-sin, zero], axis=-1), (1, reps))
    sb_t = jnp.tile(jnp.concatenate([zero, sin], axis=-1), (1, reps))
    return cos_t, sa_t, sb_t, cos.T, sin.T


def _qkv_kernel(x_ref, g_ref, wqvt_ref, wk_ref, cos_ref, sa_ref, sb_ref, cost_ref, sint_ref,
                qt_ref, k_ref, vt_ref):
    hn = _rms(x_ref[...], g_ref[...]).astype(BF16)
    half = HEAD_DIM // 2
    nq = N_HEADS * HEAD_DIM

    qvt = _dot_nt(wqvt_ref[...], hn)
    cost, sint = cost_ref[...], sint_ref[...]
    scale = HEAD_DIM ** -0.5
    for h in range(N_HEADS):
        a = qvt[h * HEAD_DIM: h * HEAD_DIM + half]
        b = qvt[h * HEAD_DIM + half: (h + 1) * HEAD_DIM]
        qt_ref[0, h * HEAD_DIM: h * HEAD_DIM + half, :] = ((a * cost - b * sint) * scale).astype(BF16)
        qt_ref[0, h * HEAD_DIM + half: (h + 1) * HEAD_DIM, :] = ((b * cost + a * sint) * scale).astype(BF16)
    vt_ref[0] = qvt[nq:].astype(BF16)

    k = _dot(hn, wk_ref[...])
    cos, sa, sb = cos_ref[...], sa_ref[...], sb_ref[...]
    for cb in range(KV_WIDTH // LANES):
        cs = slice(cb * LANES, (cb + 1) * LANES)
        xb = k[:, cs]
        r = xb * cos + pltpu.roll(xb, LANES - half, 1) * sa + pltpu.roll(xb, half, 1) * sb
        k_ref[:, cs] = r.astype(BF16)


def _qkv_layer(x2d, seq_len, g, wqvt, wk, tables):
    N, D = x2d.shape
    tile = STREAM_TILE
    tps = seq_len // tile
    B = N // seq_len
    nq = N_HEADS * HEAD_DIM
    cos_t, sa_t, sb_t, cost, sint = tables
    tab_spec = pl.BlockSpec((tile, LANES), lambda i: (i % tps, 0))
    tabt_spec = pl.BlockSpec((HEAD_DIM // 2, tile), lambda i: (0, i % tps))
    return pl.pallas_call(
        _qkv_kernel,
        grid=(N // tile,),
        in_specs=[
            pl.BlockSpec((tile, D), lambda i: (i, 0)),
            _resident((1, D)),
            _resident(wqvt.shape),
            _resident(wk.shape),
            tab_spec, tab_spec, tab_spec,
            tabt_spec, tabt_spec,
        ],
        out_specs=[
            pl.BlockSpec((1, nq, tile), lambda i: (i // tps, 0, i % tps)),
            pl.BlockSpec((tile, KV_WIDTH), lambda i: (i, 0)),
            pl.BlockSpec((1, KV_WIDTH, tile), lambda i: (i // tps, 0, i % tps)),
        ],
        out_shape=[
            jax.ShapeDtypeStruct((B, nq, seq_len), BF16),
            jax.ShapeDtypeStruct((N, KV_WIDTH), BF16),
            jax.ShapeDtypeStruct((B, KV_WIDTH, seq_len), BF16),
        ],
        compiler_params=_params(),
        name="qkv_rope",
    )(x2d, g, wqvt, wk, cos_t, sa_t, sb_t, cost, sint)


def _attend_block(qt, k, vt, sink_ref, prev_valid, next_valid):
    slab = KV_WIDTH
    qst = jnp.concatenate([qt[j * slab:(j + 1) * slab] for j in range(GQA_GROUP)], axis=1)
    lane_head = lax.shift_right_logical(
        lax.broadcasted_iota(jnp.int32, k.shape, 1), HEAD_DIM.bit_length() - 1)
    kbd = jnp.concatenate([jnp.where(lane_head == kh, k, jnp.zeros_like(k)) for kh in range(N_KV)], axis=0)
    s = _dot(kbd, qst)

    lanes = GQA_GROUP * Q_BLOCK
    r = lax.broadcasted_iota(jnp.int32, (WINDOW, lanes), 0)
    c = lax.broadcasted_iota(jnp.int32, (WINDOW, lanes), 1) & (Q_BLOCK - 1)
    prev_bias = jnp.where((r >= c) & prev_valid, 0.0, -1e30)
    next_bias = jnp.where((r <= c) & next_valid, 0.0, -1e30)

    probs, scales = [], []
    for kh in range(N_KV):
        base = kh * N_KEYS
        sk = jnp.concatenate([
            s[base:base + WINDOW] + prev_bias,
            s[base + WINDOW:base + WINDOW + Q_BLOCK],
            s[base + WINDOW + Q_BLOCK:base + N_KEYS] + next_bias], axis=0)
        sink = jnp.concatenate(
            [jnp.full((1, Q_BLOCK), sink_ref[GQA_GROUP * kh + j], F32) for j in range(GQA_GROUP)], axis=1)
        m = jnp.maximum(jnp.max(sk, axis=0, keepdims=True), sink)
        e = jnp.exp(sk - m)
        denom = jnp.sum(e, axis=0, keepdims=True) + jnp.exp(sink - m)
        probs.append(e.astype(BF16))
        scales.append(1.0 / denom)
    pt = jnp.concatenate(probs, axis=0)

    zero_v = jnp.zeros((HEAD_DIM, N_KEYS), BF16)
    vbd = jnp.concatenate(
        [jnp.concatenate([vt[kh * HEAD_DIM:(kh + 1) * HEAD_DIM] if rr == kh else zero_v
                          for rr in range(N_KV)], axis=0)
         for kh in range(N_KV)], axis=1)
    ot = _dot(vbd, pt)
    ot = jnp.concatenate([ot[kh * HEAD_DIM:(kh + 1) * HEAD_DIM] * scales[kh] for kh in range(N_KV)], axis=0)
    return jnp.concatenate([ot[:, j * Q_BLOCK:(j + 1) * Q_BLOCK] for j in range(GQA_GROUP)], axis=0)


def _winattn_kernel(sink_ref, x_ref, qt_ref, kp_ref, kc_ref, kn_ref, vp_ref, vc_ref, vn_ref, wo_ref, o_ref,
                    *, blocks):
    i = pl.program_id(1)
    last = pl.num_programs(1) - 1
    qt = qt_ref[0]
    k_all = jnp.concatenate([kp_ref[0], kc_ref[0], kn_ref[0]], axis=0)
    vt_all = jnp.concatenate([vp_ref[0], vc_ref[0], vn_ref[0]], axis=1)
    outs = []
    for blk in range(blocks):
        lo = blk * Q_BLOCK
        prev_valid = (i > 0) if blk == 0 else True
        next_valid = (i < last) if blk == blocks - 1 else True
        outs.append(_attend_block(qt[:, lo:lo + Q_BLOCK], k_all[lo:lo + N_KEYS], vt_all[:, lo:lo + N_KEYS],
                                  sink_ref, prev_valid, next_valid))
    at = jnp.concatenate(outs, axis=1).astype(BF16)
    proj = lax.dot_general(at, wo_ref[...], (((0,), (0,)), ((), ())), preferred_element_type=F32)
    o_ref[0] = x_ref[0] + proj


def _winattn_layer(x, qt, k, vt, sinks, wo):
    B, S, D = x.shape
    nq = qt.shape[1]
    blocks = ATTN_BLOCKS
    rows = blocks * Q_BLOCK
    ns = S // rows
    nb = S // Q_BLOCK
    x_spec = pl.BlockSpec((1, rows, D), lambda b, i: (b, i, 0))
    qt_spec = pl.BlockSpec((1, nq, rows), lambda b, i: (b, 0, i))
    prev = lambda i: jnp.maximum(i * blocks - 1, 0)
    nxt = lambda i: jnp.minimum((i + 1) * blocks, nb - 1)
    kern = functools.partial(_winattn_kernel, blocks=blocks)
    return pl.pallas_call(
        kern,
        grid=(B, ns),
        in_specs=[
            pl.BlockSpec(memory_space=pltpu.SMEM),
            x_spec, qt_spec,
            pl.BlockSpec((1, Q_BLOCK, KV_WIDTH), lambda b, i: (b, prev(i), 0)),
            pl.BlockSpec((1, rows, KV_WIDTH), lambda b, i: (b, i, 0)),
            pl.BlockSpec((1, Q_BLOCK, KV_WIDTH), lambda b, i: (b, nxt(i), 0)),
            pl.BlockSpec((1, KV_WIDTH, Q_BLOCK), lambda b, i: (b, 0, prev(i))),
            pl.BlockSpec((1, KV_WIDTH, rows), lambda b, i: (b, 0, i)),
            pl.BlockSpec((1, KV_WIDTH, Q_BLOCK), lambda b, i: (b, 0, nxt(i))),
            _resident(wo.shape),
        ],
        out_specs=x_spec,
        out_shape=jax.ShapeDtypeStruct((B, S, D), F32),
        compiler_params=_params(),
        name="window_attn",
    )(sinks, x, qt, k, k, k, vt, vt, vt, wo)


def _trunk(x, mem, p):
    B, S, D = x.shape
    N = B * S
    mt_mem, vw_mem = _mem_fold(mem, p["norm_mem"], p["x_wkv"], p["x_wq"], p["x_wo"])
    tables = _rope_tables(S)

    def row(a, i):
        return a[i].reshape(1, -1)

    x = _pool_layer(x, row(p["norm_mix"], 0), p["pool_w"][0], row(p["pool_scale"], 0))
    x2 = x.reshape(N, D)
    for i in range(DEPTH):
        if i % 2 == 1:
            j = i // 2
            qt, k, vt = _qkv_layer(x2, S, row(p["norm_mix"], i), p["attn_wqvt"][j], p["attn_wk"][j], tables)
            x2 = _winattn_layer(x2.reshape(B, S, D), qt, k.reshape(B, S, -1), vt,
                                p["attn_sink"][j], p["attn_o"][j]).reshape(N, D)
        x2 = _xattn_mlp_layer(x2, S, i, row(p["norm_x"], i), mt_mem, vw_mem, row(p["norm_mlp"], i),
                              p["w_up"][i], p["w_down"][i], p["norm_final"].reshape(1, -1),
                              final_norm=(i == DEPTH - 1))
    return x2.reshape(B, S, D)


def kernel(x_prompt, x_sample, mem_prompt, mem_sample, norm_mix, pool_w, pool_scale, attn_qkv, attn_o,
           attn_sink, norm_x, norm_mem, x_wq, x_wkv, x_wo, norm_mlp, w_up, w_down, norm_final):
    nq, nk = N_HEADS * HEAD_DIM, N_KV * HEAD_DIM
    perm = _head_perm()
    p = dict(
        norm_mix=norm_mix, pool_scale=pool_scale, attn_sink=attn_sink, norm_x=norm_x,
        norm_mem=norm_mem.reshape(DEPTH, 1, D_MODEL), norm_mlp=norm_mlp, norm_final=norm_final,
        pool_w=pool_w.astype(BF16),
        attn_wqvt=jnp.swapaxes(
            jnp.concatenate([attn_qkv[:, :, :nq][:, :, perm], attn_qkv[:, :, nq + nk:]], axis=2), 1, 2).astype(BF16),
        attn_wk=attn_qkv[:, :, nq:nq + nk].astype(BF16),
        attn_o=attn_o[:, perm, :].astype(BF16),
        x_wq=x_wq.astype(BF16), x_wkv=x_wkv.astype(BF16), x_wo=x_wo.astype(BF16),
        w_up=w_up.astype(BF16), w_down=w_down.astype(BF16),
    )
    return (_trunk(x_prompt, mem_prompt, p), _trunk(x_sample, mem_sample, p))
```

```python
import functools

import jax
import jax.numpy as jnp
from jax import lax
from jax.experimental import pallas as pl
from jax.experimental.pallas import tpu as pltpu

D_MODEL = 1024
DEPTH = 2
EPS = 1e-6
POOL_WINDOWS = (2, 4, 8, 16)
GROUP = D_MODEL // len(POOL_WINDOWS)
HEAD_DIM = 64
N_HEADS = 16
N_KV = 4
GQA_GROUP = N_HEADS // N_KV
WINDOW = 128
ROPE_THETA = 10000.0
N_MEM = 256
X_HEADS = 4
X_HEAD_DIM = D_MODEL // X_HEADS
D_FF = 4 * D_MODEL

LANES = 128
SUBLANES = 8
POOL_HALO = 16
POOL_PAD = SUBLANES
VMEM_LIMIT = 56 * 1024 * 1024

TOKEN_TILE = 512
STREAM_TILE = 2048
FF_CHUNK = 1024
Q_BLOCK = 128
ATTN_BLOCKS = 8
N_KEYS = Q_BLOCK + 2 * WINDOW
KV_WIDTH = N_KV * HEAD_DIM

BF16 = jnp.bfloat16
F32 = jnp.float32


def _params():
    return pltpu.CompilerParams(vmem_limit_bytes=VMEM_LIMIT)


def _resident(shape):
    zeros = (0,) * len(shape)
    return pl.BlockSpec(shape, lambda *_: zeros, pipeline_mode=pl.Buffered(1))


def _rms(x, g):
    ms = jnp.mean(x * x, axis=-1, keepdims=True)
    return (x * lax.rsqrt(ms + EPS)) * g


def _dot(a, b):
    return jnp.dot(a, b, preferred_element_type=F32)


def _dot_nt(a, b):
    return lax.dot_general(a, b, (((1,), (1,)), ((), ())), preferred_element_type=F32)


def _pool_kernel(xc_ref, xp_ref, xn_ref, g_ref, w_ref, sc_ref, o_ref, hbuf, s1, s2, *, seq_len, tile):
    i = pl.program_id(1)
    last = pl.num_programs(1) - 1
    g = g_ref[...]
    x = xc_ref[0]
    hn = _rms(x, g)
    hp = _rms(xp_ref[0], g) * jnp.where(i > 0, 1.0, 0.0)
    hx = _rms(xn_ref[0], g) * jnp.where(i < last, 1.0, 0.0)
    H = POOL_HALO
    hbuf[0:H, :] = hp
    hbuf[H:H + tile, :] = hn
    hbuf[H + tile:H + tile + H, :] = hx
    hbuf[H + tile + H:, :] = jnp.zeros((POOL_PAD, hbuf.shape[1]), F32)

    n8 = tile + H
    n4 = n8 + POOL_PAD
    n2 = n4 + POOL_PAD
    t = i * tile + lax.broadcasted_iota(jnp.int32, (tile, 1), 0)
    left = jnp.minimum(t, H).astype(F32)
    right = jnp.minimum(seq_len - t, H).astype(F32)
    for gi, w in enumerate(POOL_WINDOWS):
        cs = slice(gi * GROUP, (gi + 1) * GROUP)
        if w == 2:
            total = hbuf[pl.ds(H - 1, tile), cs] + hbuf[pl.ds(H, tile), cs]
        else:
            s1[0:n2, :] = hbuf[0:n2, cs] + hbuf[1:n2 + 1, cs]
            if w == 4:
                total = s1[pl.ds(H - 2, tile), :] + s1[pl.ds(H, tile), :]
            else:
                s2[0:n4, :] = s1[0:n4, :] + s1[2:n4 + 2, :]
                if w == 8:
                    total = s2[pl.ds(H - 4, tile), :] + s2[pl.ds(H, tile), :]
                else:
                    s1[0:n8, :] = s2[0:n8, :] + s2[4:n8 + 4, :]
                    total = s1[pl.ds(H - 8, tile), :] + s1[pl.ds(H, tile), :]
        count = jnp.minimum(left, float(w // 2)) + jnp.minimum(right, float(w // 2))
        diff = total * (1.0 / count) - hn[:, cs]
        y = _dot(diff.astype(BF16), w_ref[gi])
        o_ref[0, :, cs] = x[:, cs] + y * sc_ref[:, cs]


def _pool_layer(x, g, w, scale):
    B, S, D = x.shape
    tile = STREAM_TILE
    nt = S // tile
    hb = tile // POOL_HALO
    nhb = S // POOL_HALO
    kern = functools.partial(_pool_kernel, seq_len=S, tile=tile)
    return pl.pallas_call(
        kern,
        grid=(B, nt),
        in_specs=[
            pl.BlockSpec((1, tile, D), lambda b, i: (b, i, 0)),
            pl.BlockSpec((1, POOL_HALO, D), lambda b, i: (b, jnp.maximum(i * hb - 1, 0), 0)),
            pl.BlockSpec((1, POOL_HALO, D), lambda b, i: (b, jnp.minimum((i + 1) * hb, nhb - 1), 0)),
            _resident((1, D)),
            _resident(w.shape),
            _resident((1, D)),
        ],
        out_specs=pl.BlockSpec((1, tile, D), lambda b, i: (b, i, 0)),
        out_shape=jax.ShapeDtypeStruct(x.shape, F32),
        scratch_shapes=[pltpu.VMEM((tile + 2 * POOL_HALO + POOL_PAD, D), F32),
                        pltpu.VMEM((tile + 2 * POOL_HALO, GROUP), F32),
                        pltpu.VMEM((tile + 2 * POOL_HALO, GROUP), F32)],
        compiler_params=_params(),
        name="pool_mixer",
    )(x, x, x, g, w, scale)


def _memfold_kernel(m_ref, g_ref, wkv_ref, wq_ref, wo_ref, mt_ref, vw_ref):
    mn = _rms(m_ref[0], g_ref[0]).astype(BF16)
    kv = _dot(mn, wkv_ref[0])
    k = kv[:, :D_MODEL].astype(BF16)
    v = kv[:, D_MODEL:].astype(BF16)
    for h in range(X_HEADS):
        cs = slice(h * X_HEAD_DIM, (h + 1) * X_HEAD_DIM)
        rs = slice(h * N_MEM, (h + 1) * N_MEM)
        mt = _dot_nt(k[:, cs], wq_ref[0, :, cs]) * (X_HEAD_DIM ** -0.5)
        mt_ref[0, 0, rs, :] = mt.astype(BF16)
        vw_ref[0, 0, rs, :] = _dot(v[:, cs], wo_ref[0, cs, :]).astype(BF16)


def _mem_fold(mem, g, w_kv, w_q, w_o):
    B = mem.shape[0]
    rows = X_HEADS * N_MEM
    out = jax.ShapeDtypeStruct((DEPTH, B, rows, D_MODEL), BF16)
    w_spec = pl.BlockSpec((1, D_MODEL, D_MODEL), lambda l, b: (l, 0, 0))
    o_spec = pl.BlockSpec((1, 1, rows, D_MODEL), lambda l, b: (l, b, 0, 0))
    return pl.pallas_call(
        _memfold_kernel,
        grid=(DEPTH, B),
        in_specs=[
            pl.BlockSpec((1, N_MEM, D_MODEL), lambda l, b: (b, 0, 0)),
            pl.BlockSpec((1, 1, D_MODEL), lambda l, b: (l, 0, 0)),
            pl.BlockSpec((1, D_MODEL, 2 * D_MODEL), lambda l, b: (l, 0, 0)),
            w_spec, w_spec,
        ],
        out_specs=[o_spec, o_spec],
        out_shape=[out, out],
        compiler_params=_params(),
        name="mem_fold",
    )(mem, g, w_kv, w_q, w_o)


def _xattn_mlp_kernel(x_ref, gx_ref, mt_ref, vw_ref, gm_ref, wu_ref, wd_ref, gf_ref, o_ref, *, final_norm):
    x = x_ref[...]
    hn = _rms(x, gx_ref[...]).astype(BF16)
    st = _dot_nt(mt_ref[0, 0], hn)
    probs = []
    for h in range(X_HEADS):
        s = st[h * N_MEM:(h + 1) * N_MEM]
        m = jnp.max(s, axis=0, keepdims=True)
        e = jnp.exp(s - m)
        r = 1.0 / jnp.sum(e, axis=0, keepdims=True)
        probs.append((e * r).astype(BF16))
    pt = jnp.concatenate(probs, axis=0)
    x = x + lax.dot_general(pt, vw_ref[0, 0], (((0,), (0,)), ((), ())), preferred_element_type=F32)

    hn = _rms(x, gm_ref[...]).astype(BF16)
    acc = x
    for c in range(D_FF // FF_CHUNK):
        cs = slice(c * FF_CHUNK, (c + 1) * FF_CHUNK)
        u = jnp.maximum(_dot(hn, wu_ref[:, cs]), 0.0)
        acc = acc + _dot((u * u).astype(BF16), wd_ref[cs, :])
    if final_norm:
        acc = _rms(acc, gf_ref[...])
    o_ref[...] = acc


def _xattn_mlp_layer(x2d, seq_len, layer, gx, mt, vw, gm, w_up, w_down, g_final, final_norm):
    N, D = x2d.shape
    tile = TOKEN_TILE
    tps = seq_len // tile
    fold_spec = pl.BlockSpec((1, 1, X_HEADS * N_MEM, D), lambda i: (layer, i // tps, 0, 0))
    kern = functools.partial(_xattn_mlp_kernel, final_norm=final_norm)
    return pl.pallas_call(
        kern,
        grid=(N // tile,),
        in_specs=[
            pl.BlockSpec((tile, D), lambda i: (i, 0)),
            _resident((1, D)),
            fold_spec,
            fold_spec,
            _resident((1, D)),
            _resident((D, D_FF)),
            _resident((D_FF, D)),
            _resident((1, D)),
        ],
        out_specs=pl.BlockSpec((tile, D), lambda i: (i, 0)),
        out_shape=jax.ShapeDtypeStruct((N, D), F32),
        compiler_params=_params(),
        name="xattn_mlp",
    )(x2d, gx, mt, vw, gm, w_up, w_down, g_final)


def _head_perm():
    idx = []
    for j in range(GQA_GROUP):
        for kh in range(N_KV):
            h = GQA_GROUP * kh + j
            idx.extend(range(h * HEAD_DIM, (h + 1) * HEAD_DIM))
    return jnp.asarray(idx, dtype=jnp.int32)


def _rope_tables(seq_len):
    inv_freq = ROPE_THETA ** (-jnp.arange(0, HEAD_DIM, 2, dtype=F32) / HEAD_DIM)
    ang = jnp.arange(seq_len, dtype=F32)[:, None] * inv_freq[None, :]
    cos, sin = jnp.cos(ang), jnp.sin(ang)
    zero = jnp.zeros_like(sin)
    reps = LANES // HEAD_DIM
    cos_t = jnp.tile(jnp.concatenate([cos, cos], axis=-1), (1, reps))
    sa_t = jnp.tile(jnp.concatenate([-sin, zero], axis=-1), (1, reps))
    sb_t = jnp.tile(jnp.concatenate([zero, sin], axis=-1), (1, reps))
    return cos_t, sa_t, sb_t, cos.T, sin.T


def _qkv_kernel(x_ref, g_ref, wqvt_ref, wk_ref, cos_ref, sa_ref, sb_ref, cost_ref, sint_ref,
                qt_ref, k_ref, vt_ref):
    hn = _rms(x_ref[...], g_ref[...]).astype(BF16)
    half = HEAD_DIM // 2
    nq = N_HEADS * HEAD_DIM

    qvt = _dot_nt(wqvt_ref[...], hn)
    cost, sint = cost_ref[...], sint_ref[...]
    scale = HEAD_DIM ** -0.5
    for h in range(N_HEADS):
        a = qvt[h * HEAD_DIM: h * HEAD_DIM + half]
        b = qvt[h * HEAD_DIM + half: (h + 1) * HEAD_DIM]
        qt_ref[0, h * HEAD_DIM: h * HEAD_DIM + half, :] = ((a * cost - b * sint) * scale).astype(BF16)
        qt_ref[0, h * HEAD_DIM + half: (h + 1) * HEAD_DIM, :] = ((b * cost + a * sint) * scale).astype(BF16)
    vt_ref[0] = qvt[nq:].astype(BF16)

    k = _dot(hn, wk_ref[...])
    cos, sa, sb = cos_ref[...], sa_ref[...], sb_ref[...]
    for cb in range(KV_WIDTH // LANES):
        cs = slice(cb * LANES, (cb + 1) * LANES)
        xb = k[:, cs]
        r = xb * cos + pltpu.roll(xb, LANES - half, 1) * sa + pltpu.roll(xb, half, 1) * sb
        k_ref[:, cs] = r.astype(BF16)


def _qkv_layer(x2d, seq_len, g, wqvt, wk, tables):
    N, D = x2d.shape
    tile = STREAM_TILE
    tps = seq_len // tile
    B = N // seq_len
    nq = N_HEADS * HEAD_DIM
    cos_t, sa_t, sb_t, cost, sint = tables
    tab_spec = pl.BlockSpec((tile, LANES), lambda i: (i % tps, 0))
    tabt_spec = pl.BlockSpec((HEAD_DIM // 2, tile), lambda i: (0, i % tps))
    return pl.pallas_call(
        _qkv_kernel,
        grid=(N // tile,),
        in_specs=[
            pl.BlockSpec((tile, D), lambda i: (i, 0)),
            _resident((1, D)),
            _resident(wqvt.shape),
            _resident(wk.shape),
            tab_spec, tab_spec, tab_spec,
            tabt_spec, tabt_spec,
        ],
        out_specs=[
            pl.BlockSpec((1, nq, tile), lambda i: (i // tps, 0, i % tps)),
            pl.BlockSpec((tile, KV_WIDTH), lambda i: (i, 0)),
            pl.BlockSpec((1, KV_WIDTH, tile), lambda i: (i // tps, 0, i % tps)),
        ],
        out_shape=[
            jax.ShapeDtypeStruct((B, nq, seq_len), BF16),
            jax.ShapeDtypeStruct((N, KV_WIDTH), BF16),
            jax.ShapeDtypeStruct((B, KV_WIDTH, seq_len), BF16),
        ],
        compiler_params=_params(),
        name="qkv_rope",
    )(x2d, g, wqvt, wk, cos_t, sa_t, sb_t, cost, sint)


def _attend_block(qt, k, vt, sink_ref, prev_valid, next_valid):
    slab = KV_WIDTH
    qst = jnp.concatenate([qt[j * slab:(j + 1) * slab] for j in range(GQA_GROUP)], axis=1)
    lane_head = lax.shift_right_logical(
        lax.broadcasted_iota(jnp.int32, k.shape, 1), HEAD_DIM.bit_length() - 1)
    kbd = jnp.concatenate([jnp.where(lane_head == kh, k, jnp.zeros_like(k)) for kh in range(N_KV)], axis=0)
    s = _dot(kbd, qst)

    lanes = GQA_GROUP * Q_BLOCK
    r = lax.broadcasted_iota(jnp.int32, (WINDOW, lanes), 0)
    c = lax.broadcasted_iota(jnp.int32, (WINDOW, lanes), 1) & (Q_BLOCK - 1)
    prev_bias = jnp.where((r >= c) & prev_valid, 0.0, -1e30)
    next_bias = jnp.where((r <= c) & next_valid, 0.0, -1e30)

    probs, scales = [], []
    for kh in range(N_KV):
        base = kh * N_KEYS
        sk = jnp.concatenate([
            s[base:base + WINDOW] + prev_bias,
            s[base + WINDOW:base + WINDOW + Q_BLOCK],
            s[base + WINDOW + Q_BLOCK:base + N_KEYS] + next_bias], axis=0)
        sink = jnp.concatenate(
            [jnp.full((1, Q_BLOCK), sink_ref[GQA_GROUP * kh + j], F32) for j in range(GQA_GROUP)], axis=1)
        m = jnp.maximum(jnp.max(sk, axis=0, keepdims=True), sink)
        e = jnp.exp(sk - m)
        denom = jnp.sum(e, axis=0, keepdims=True) + jnp.exp(sink - m)
        probs.append(e.astype(BF16))
        scales.append(1.0 / denom)
    pt = jnp.concatenate(probs, axis=0)

    zero_v = jnp.zeros((HEAD_DIM, N_KEYS), BF16)
    vbd = jnp.concatenate(
        [jnp.concatenate([vt[kh * HEAD_DIM:(kh + 1) * HEAD_DIM] if rr == kh else zero_v
                          for rr in range(N_KV)], axis=0)
         for kh in range(N_KV)], axis=1)
    ot = _dot(vbd, pt)
    ot = jnp.concatenate([ot[kh * HEAD_DIM:(kh + 1) * HEAD_DIM] * scales[kh] for kh in range(N_KV)], axis=0)
    return jnp.concatenate([ot[:, j * Q_BLOCK:(j + 1) * Q_BLOCK] for j in range(GQA_GROUP)], axis=0)


def _winattn_kernel(sink_ref, x_ref, qt_ref, kp_ref, kc_ref, kn_ref, vp_ref, vc_ref, vn_ref, wo_ref, o_ref,
                    *, blocks):
    i = pl.program_id(1)
    last = pl.num_programs(1) - 1
    qt = qt_ref[0]
    k_all = jnp.concatenate([kp_ref[0], kc_ref[0], kn_ref[0]], axis=0)
    vt_all = jnp.concatenate([vp_ref[0], vc_ref[0], vn_ref[0]], axis=1)
    outs = []
    for blk in range(blocks):
        lo = blk * Q_BLOCK
        prev_valid = (i > 0) if blk == 0 else True
        next_valid = (i < last) if blk == blocks - 1 else True
        outs.append(_attend_block(qt[:, lo:lo + Q_BLOCK], k_all[lo:lo + N_KEYS], vt_all[:, lo:lo + N_KEYS],
                                  sink_ref, prev_valid, next_valid))
    at = jnp.concatenate(outs, axis=1).astype(BF16)
    proj = lax.dot_general(at, wo_ref[...], (((0,), (0,)), ((), ())), preferred_element_type=F32)
    o_ref[0] = x_ref[0] + proj


def _winattn_layer(x, qt, k, vt, sinks, wo):
    B, S, D = x.shape
    nq = qt.shape[1]
    blocks = ATTN_BLOCKS
    rows = blocks * Q_BLOCK
    ns = S // rows
    nb = S // Q_BLOCK
    x_spec = pl.BlockSpec((1, rows, D), lambda b, i: (b, i, 0))
    qt_spec = pl.BlockSpec((1, nq, rows), lambda b, i: (b, 0, i))
    prev = lambda i: jnp.maximum(i * blocks - 1, 0)
    nxt = lambda i: jnp.minimum((i + 1) * blocks, nb - 1)
    kern = functools.partial(_winattn_kernel, blocks=blocks)
    return pl.pallas_call(
        kern,
        grid=(B, ns),
        in_specs=[
            pl.BlockSpec(memory_space=pltpu.SMEM),
            x_spec, qt_spec,
            pl.BlockSpec((1, Q_BLOCK, KV_WIDTH), lambda b, i: (b, prev(i), 0)),
            pl.BlockSpec((1, rows, KV_WIDTH), lambda b, i: (b, i, 0)),
            pl.BlockSpec((1, Q_BLOCK, KV_WIDTH), lambda b, i: (b, nxt(i), 0)),
            pl.BlockSpec((1, KV_WIDTH, Q_BLOCK), lambda b, i: (b, 0, prev(i))),
            pl.BlockSpec((1, KV_WIDTH, rows), lambda b, i: (b, 0, i)),
            pl.BlockSpec((1, KV_WIDTH, Q_BLOCK), lambda b, i: (b, 0, nxt(i))),
            _resident(wo.shape),
        ],
        out_specs=x_spec,
        out_shape=jax.ShapeDtypeStruct((B, S, D), F32),
        compiler_params=_params(),
        name="window_attn",
    )(sinks, x, qt, k, k, k, vt, vt, vt, wo)


def _trunk(x, mem, p):
    B, S, D = x.shape
    N = B * S
    mt_mem, vw_mem = _mem_fold(mem, p["norm_mem"], p["x_wkv"], p["x_wq"], p["x_wo"])
    tables = _rope_tables(S)

    def row(a, i):
        return a[i].reshape(1, -1)

    x = _pool_layer(x, row(p["norm_mix"], 0), p["pool_w"][0], row(p["pool_scale"], 0))
    x2 = x.reshape(N, D)
    for i in range(DEPTH):
        if i % 2 == 1:
            j = i // 2
            qt, k, vt = _qkv_layer(x2, S, row(p["norm_mix"], i), p["attn_wqvt"][j], p["attn_wk"][j], tables)
            x2 = _winattn_layer(x2.reshape(B, S, D), qt, k.reshape(B, S, -1), vt,
                                p["attn_sink"][j], p["attn_o"][j]).reshape(N, D)
        x2 = _xattn_mlp_layer(x2, S, i, row(p["norm_x"], i), mt_mem, vw_mem, row(p["norm_mlp"], i),
                              p["w_up"][i], p["w_down"][i], p["norm_final"].reshape(1, -1),
                              final_norm=(i == DEPTH - 1))
    return x2.reshape(B, S, D)


def kernel(x_prompt, x_sample, mem_prompt, mem_sample, norm_mix, pool_w, pool_scale, attn_qkv, attn_o,
           attn_sink, norm_x, norm_mem, x_wq, x_wkv, x_wo, norm_mlp, w_up, w_down, norm_final):
    nq, nk = N_HEADS * HEAD_DIM, N_KV * HEAD_DIM
    perm = _head_perm()
    p = dict(
        norm_mix=norm_mix, pool_scale=pool_scale, attn_sink=attn_sink, norm_x=norm_x,
        norm_mem=norm_mem.reshape(DEPTH, 1, D_MODEL), norm_mlp=norm_mlp, norm_final=norm_final,
        pool_w=pool_w.astype(BF16),
        attn_wqvt=jnp.swapaxes(
            jnp.concatenate([attn_qkv[:, :, :nq][:, :, perm], attn_qkv[:, :, nq + nk:]], axis=2), 1, 2).astype(BF16),
        attn_wk=attn_qkv[:, :, nq:nq + nk].astype(BF16),
        attn_o=attn_o[:, perm, :].astype(BF16),
        x_wq=x_wq.astype(BF16), x_wkv=x_wkv.astype(BF16), x_wo=x_wo.astype(BF16),
        w_up=w_up.astype(BF16), w_down=w_down.astype(BF16),
    )
    return (_trunk(x_prompt, mem_prompt, p), _trunk(x_sample, mem_sample, p))
```

```python
import functools

import jax
import jax.numpy as jnp
from jax import lax
from jax.experimental import pallas as pl
from jax.experimental.pallas import tpu as pltpu

D_MODEL = 1024
DEPTH = 2
EPS = 1e-6
POOL_WINDOWS = (2, 4, 8, 16)
GROUP = D_MODEL // len(POOL_WINDOWS)
HEAD_DIM = 64
N_HEADS = 16
N_KV = 4
GQA_GROUP = N_HEADS // N_KV
WINDOW = 128
ROPE_THETA = 10000.0
LOG2E = 1.4426950408889634
N_MEM = 256
X_HEADS = 4
X_HEAD_DIM = D_MODEL // X_HEADS
D_FF = 4 * D_MODEL

LANES = 128
SUBLANES = 8
POOL_HALO = 16
POOL_PAD = SUBLANES
VMEM_LIMIT = 56 * 1024 * 1024

TOKEN_TILE = 512
STREAM_TILE = 2048
FF_CHUNK = 1024
Q_BLOCK = 128
ATTN_BLOCKS = 8
N_KEYS = Q_BLOCK + 2 * WINDOW
KV_WIDTH = N_KV * HEAD_DIM

BF16 = jnp.bfloat16
F32 = jnp.float32


def _params():
    return pltpu.CompilerParams(vmem_limit_bytes=VMEM_LIMIT)


def _resident(shape):
    zeros = (0,) * len(shape)
    return pl.BlockSpec(shape, lambda *_: zeros, pipeline_mode=pl.Buffered(1))


def _rms(x, g):
    ms = jnp.mean(x * x, axis=-1, keepdims=True)
    return (x * lax.rsqrt(ms + EPS)) * g


def _dot(a, b):
    return jnp.dot(a, b, preferred_element_type=F32)


def _dot_nt(a, b):
    return lax.dot_general(a, b, (((1,), (1,)), ((), ())), preferred_element_type=F32)


def _pool_kernel(xc_ref, xp_ref, xn_ref, g_ref, w_ref, sc_ref, o_ref, hbuf, s1, s2, *, seq_len, tile):
    i = pl.program_id(1)
    last = pl.num_programs(1) - 1
    g = g_ref[...]
    x = xc_ref[0]
    hn = _rms(x, g)
    hp = _rms(xp_ref[0], g) * jnp.where(i > 0, 1.0, 0.0)
    hx = _rms(xn_ref[0], g) * jnp.where(i < last, 1.0, 0.0)
    H = POOL_HALO
    hbuf[0:H, :] = hp
    hbuf[H:H + tile, :] = hn
    hbuf[H + tile:H + tile + H, :] = hx
    hbuf[H + tile + H:, :] = jnp.zeros((POOL_PAD, hbuf.shape[1]), F32)

    n8 = tile + H
    n4 = n8 + POOL_PAD
    n2 = n4 + POOL_PAD
    t = i * tile + lax.broadcasted_iota(jnp.int32, (tile, 1), 0)
    left = jnp.minimum(t, H).astype(F32)
    right = jnp.minimum(seq_len - t, H).astype(F32)
    for gi, w in enumerate(POOL_WINDOWS):
        cs = slice(gi * GROUP, (gi + 1) * GROUP)
        if w == 2:
            total = hbuf[pl.ds(H - 1, tile), cs] + hbuf[pl.ds(H, tile), cs]
        else:
            s1[0:n2, :] = hbuf[0:n2, cs] + hbuf[1:n2 + 1, cs]
            if w == 4:
                total = s1[pl.ds(H - 2, tile), :] + s1[pl.ds(H, tile), :]
            else:
                s2[0:n4, :] = s1[0:n4, :] + s1[2:n4 + 2, :]
                if w == 8:
                    total = s2[pl.ds(H - 4, tile), :] + s2[pl.ds(H, tile), :]
                else:
                    s1[0:n8, :] = s2[0:n8, :] + s2[4:n8 + 4, :]
                    total = s1[pl.ds(H - 8, tile), :] + s1[pl.ds(H, tile), :]
        count = jnp.minimum(left, float(w // 2)) + jnp.minimum(right, float(w // 2))
        diff = total * (1.0 / count) - hn[:, cs]
        y = _dot(diff.astype(BF16), w_ref[gi])
        o_ref[0, :, cs] = x[:, cs] + y * sc_ref[:, cs]


def _pool_layer(x, g, w, scale):
    B, S, D = x.shape
    tile = STREAM_TILE
    nt = S // tile
    hb = tile // POOL_HALO
    nhb = S // POOL_HALO
    kern = functools.partial(_pool_kernel, seq_len=S, tile=tile)
    return pl.pallas_call(
        kern,
        grid=(B, nt),
        in_specs=[
            pl.BlockSpec((1, tile, D), lambda b, i: (b, i, 0)),
            pl.BlockSpec((1, POOL_HALO, D), lambda b, i: (b, jnp.maximum(i * hb - 1, 0), 0)),
            pl.BlockSpec((1, POOL_HALO, D), lambda b, i: (b, jnp.minimum((i + 1) * hb, nhb - 1), 0)),
            _resident((1, D)),
            _resident(w.shape),
            _resident((1, D)),
        ],
        out_specs=pl.BlockSpec((1, tile, D), lambda b, i: (b, i, 0)),
        out_shape=jax.ShapeDtypeStruct(x.shape, F32),
        scratch_shapes=[pltpu.VMEM((tile + 2 * POOL_HALO + POOL_PAD, D), F32),
                        pltpu.VMEM((tile + 2 * POOL_HALO, GROUP), F32),
                        pltpu.VMEM((tile + 2 * POOL_HALO, GROUP), F32)],
        compiler_params=_params(),
        name="pool_mixer",
    )(x, x, x, g, w, scale)


def _memfold_kernel(m_ref, g_ref, wkv_ref, wq_ref, wo_ref, mt_ref, vw_ref):
    mn = _rms(m_ref[0], g_ref[0]).astype(BF16)
    kv = _dot(mn, wkv_ref[0])
    k = kv[:, :D_MODEL].astype(BF16)
    v = kv[:, D_MODEL:].astype(BF16)
    for h in range(X_HEADS):
        cs = slice(h * X_HEAD_DIM, (h + 1) * X_HEAD_DIM)
        rs = slice(h * N_MEM, (h + 1) * N_MEM)
        mt = _dot_nt(k[:, cs], wq_ref[0, :, cs]) * (X_HEAD_DIM ** -0.5)
        mt_ref[0, 0, rs, :] = mt.astype(BF16)
        vw_ref[0, 0, rs, :] = _dot(v[:, cs], wo_ref[0, cs, :]).astype(BF16)


def _mem_fold(mem, g, w_kv, w_q, w_o):
    B = mem.shape[0]
    rows = X_HEADS * N_MEM
    out = jax.ShapeDtypeStruct((DEPTH, B, rows, D_MODEL), BF16)
    w_spec = pl.BlockSpec((1, D_MODEL, D_MODEL), lambda l, b: (l, 0, 0))
    o_spec = pl.BlockSpec((1, 1, rows, D_MODEL), lambda l, b: (l, b, 0, 0))
    return pl.pallas_call(
        _memfold_kernel,
        grid=(DEPTH, B),
        in_specs=[
            pl.BlockSpec((1, N_MEM, D_MODEL), lambda l, b: (b, 0, 0)),
            pl.BlockSpec((1, 1, D_MODEL), lambda l, b: (l, 0, 0)),
            pl.BlockSpec((1, D_MODEL, 2 * D_MODEL), lambda l, b: (l, 0, 0)),
            w_spec, w_spec,
        ],
        out_specs=[o_spec, o_spec],
        out_shape=[out, out],
        compiler_params=_params(),
        name="mem_fold",
    )(mem, g, w_kv, w_q, w_o)


def _xattn_mlp_kernel(x_ref, gx_ref, mt_ref, vw_ref, gm_ref, wu_ref, wd_ref, gf_ref, o_ref, *, final_norm):
    x = x_ref[...]
    hn = _rms(x, gx_ref[...]).astype(BF16)
    st = _dot_nt(mt_ref[0, 0], hn)
    probs = []
    for h in range(X_HEADS):
        s = st[h * N_MEM:(h + 1) * N_MEM]
        m = jnp.max(s, axis=0, keepdims=True)
        e = jnp.exp(s - m)
        r = 1.0 / jnp.sum(e, axis=0, keepdims=True)
        probs.append((e * r).astype(BF16))
    pt = jnp.concatenate(probs, axis=0)
    x = x + lax.dot_general(pt, vw_ref[0, 0], (((0,), (0,)), ((), ())), preferred_element_type=F32)

    hn = _rms(x, gm_ref[...]).astype(BF16)
    acc = x
    for c in range(D_FF // FF_CHUNK):
        cs = slice(c * FF_CHUNK, (c + 1) * FF_CHUNK)
        u = jnp.maximum(_dot(hn, wu_ref[:, cs]), 0.0)
        acc = acc + _dot((u * u).astype(BF16), wd_ref[cs, :])
    if final_norm:
        acc = _rms(acc, gf_ref[...])
    o_ref[...] = acc


def _xattn_mlp_layer(x2d, seq_len, layer, gx, mt, vw, gm, w_up, w_down, g_final, final_norm):
    N, D = x2d.shape
    tile = TOKEN_TILE
    tps = seq_len // tile
    fold_spec = pl.BlockSpec((1, 1, X_HEADS * N_MEM, D), lambda i: (layer, i // tps, 0, 0))
    kern = functools.partial(_xattn_mlp_kernel, final_norm=final_norm)
    return pl.pallas_call(
        kern,
        grid=(N // tile,),
        in_specs=[
            pl.BlockSpec((tile, D), lambda i: (i, 0)),
            _resident((1, D)),
            fold_spec,
            fold_spec,
            _resident((1, D)),
            _resident((D, D_FF)),
            _resident((D_FF, D)),
            _resident((1, D)),
        ],
        out_specs=pl.BlockSpec((tile, D), lambda i: (i, 0)),
        out_shape=jax.ShapeDtypeStruct((N, D), F32),
        compiler_params=_params(),
        name="xattn_mlp",
    )(x2d, gx, mt, vw, gm, w_up, w_down, g_final)


def _head_perm():
    idx = []
    for j in range(GQA_GROUP):
        for kh in range(N_KV):
            h = GQA_GROUP * kh + j
            idx.extend(range(h * HEAD_DIM, (h + 1) * HEAD_DIM))
    return jnp.asarray(idx, dtype=jnp.int32)


def _rope_tables(seq_len):
    inv_freq = ROPE_THETA ** (-jnp.arange(0, HEAD_DIM, 2, dtype=F32) / HEAD_DIM)
    ang = jnp.arange(seq_len, dtype=F32)[:, None] * inv_freq[None, :]
    cos, sin = jnp.cos(ang), jnp.sin(ang)
    zero = jnp.zeros_like(sin)
    reps = LANES // HEAD_DIM
    cos_t = jnp.tile(jnp.concatenate([cos, cos], axis=-1), (1, reps))
    sa_t = jnp.tile(jnp.concatenate([-sin, zero], axis=-1), (1, reps))
    sb_t = jnp.tile(jnp.concatenate([zero, sin], axis=-1), (1, reps))
    return cos_t, sa_t, sb_t, cos.T, sin.T


def _qkv_kernel(x_ref, g_ref, wqvt_ref, wk_ref, cos_ref, sa_ref, sb_ref, cost_ref, sint_ref,
                qt_ref, k_ref, vt_ref):
    hn = _rms(x_ref[...], g_ref[...]).astype(BF16)
    half = HEAD_DIM // 2
    nq = N_HEADS * HEAD_DIM

    qvt = _dot_nt(wqvt_ref[...], hn)
    cost, sint = cost_ref[...], sint_ref[...]
    scale = HEAD_DIM ** -0.5 * LOG2E
    for h in range(N_HEADS):
        a = qvt[h * HEAD_DIM: h * HEAD_DIM + half]
        b = qvt[h * HEAD_DIM + half: (h + 1) * HEAD_DIM]
        qt_ref[0, h * HEAD_DIM: h * HEAD_DIM + half, :] = ((a * cost - b * sint) * scale).astype(BF16)
        qt_ref[0, h * HEAD_DIM + half: (h + 1) * HEAD_DIM, :] = ((b * cost + a * sint) * scale).astype(BF16)
    vt_ref[0] = qvt[nq:].astype(BF16)

    k = _dot(hn, wk_ref[...])
    cos, sa, sb = cos_ref[...], sa_ref[...], sb_ref[...]
    for cb in range(KV_WIDTH // LANES):
        cs = slice(cb * LANES, (cb + 1) * LANES)
        xb = k[:, cs]
        r = xb * cos + pltpu.roll(xb, LANES - half, 1) * sa + pltpu.roll(xb, half, 1) * sb
        k_ref[:, cs] = r.astype(BF16)


def _qkv_layer(x2d, seq_len, g, wqvt, wk, tables):
    N, D = x2d.shape
    tile = STREAM_TILE
    tps = seq_len // tile
    B = N // seq_len
    nq = N_HEADS * HEAD_DIM
    cos_t, sa_t, sb_t, cost, sint = tables
    tab_spec = pl.BlockSpec((tile, LANES), lambda i: (i % tps, 0))
    tabt_spec = pl.BlockSpec((HEAD_DIM // 2, tile), lambda i: (0, i % tps))
    return pl.pallas_call(
        _qkv_kernel,
        grid=(N // tile,),
        in_specs=[
            pl.BlockSpec((tile, D), lambda i: (i, 0)),
            _resident((1, D)),
            _resident(wqvt.shape),
            _resident(wk.shape),
            tab_spec, tab_spec, tab_spec,
            tabt_spec, tabt_spec,
        ],
        out_specs=[
            pl.BlockSpec((1, nq, tile), lambda i: (i // tps, 0, i % tps)),
            pl.BlockSpec((tile, KV_WIDTH), lambda i: (i, 0)),
            pl.BlockSpec((1, KV_WIDTH, tile), lambda i: (i // tps, 0, i % tps)),
        ],
        out_shape=[
            jax.ShapeDtypeStruct((B, nq, seq_len), BF16),
            jax.ShapeDtypeStruct((N, KV_WIDTH), BF16),
            jax.ShapeDtypeStruct((B, KV_WIDTH, seq_len), BF16),
        ],
        compiler_params=_params(),
        name="qkv_rope",
    )(x2d, g, wqvt, wk, cos_t, sa_t, sb_t, cost, sint)


def _attend_block(qt, k, vt, sink_ref, prev_valid, next_valid):
    slab = KV_WIDTH
    qst = jnp.concatenate([qt[j * slab:(j + 1) * slab] for j in range(GQA_GROUP)], axis=1)
    lane_head = lax.shift_right_logical(
        lax.broadcasted_iota(jnp.int32, k.shape, 1), HEAD_DIM.bit_length() - 1)
    kbd = jnp.concatenate([jnp.where(lane_head == kh, k, jnp.zeros_like(k)) for kh in range(N_KV)], axis=0)
    s = _dot(kbd, qst)

    lanes = GQA_GROUP * Q_BLOCK
    r = lax.broadcasted_iota(jnp.int32, (WINDOW, lanes), 0)
    c = lax.broadcasted_iota(jnp.int32, (WINDOW, lanes), 1) & (Q_BLOCK - 1)
    prev_bias = jnp.where((r >= c) & prev_valid, 0.0, -1e30)
    next_bias = jnp.where((r <= c) & next_valid, 0.0, -1e30)

    probs, scales = [], []
    for kh in range(N_KV):
        base = kh * N_KEYS
        sk = jnp.concatenate([
            s[base:base + WINDOW] + prev_bias,
            s[base + WINDOW:base + WINDOW + Q_BLOCK],
            s[base + WINDOW + Q_BLOCK:base + N_KEYS] + next_bias], axis=0)
        sink = jnp.concatenate(
            [jnp.full((1, Q_BLOCK), sink_ref[GQA_GROUP * kh + j] * LOG2E, F32) for j in range(GQA_GROUP)], axis=1)
        m = jnp.maximum(jnp.max(sk, axis=0, keepdims=True), sink)
        e = jnp.exp2(sk - m)
        denom = jnp.sum(e, axis=0, keepdims=True) + jnp.exp2(sink - m)
        probs.append(e.astype(BF16))
        scales.append(1.0 / denom)
    pt = jnp.concatenate(probs, axis=0)

    zero_v = jnp.zeros((HEAD_DIM, N_KEYS), BF16)
    vbd = jnp.concatenate(
        [jnp.concatenate([vt[kh * HEAD_DIM:(kh + 1) * HEAD_DIM] if rr == kh else zero_v
                          for rr in range(N_KV)], axis=0)
         for kh in range(N_KV)], axis=1)
    ot = _dot(vbd, pt)
    ot = jnp.concatenate([ot[kh * HEAD_DIM:(kh + 1) * HEAD_DIM] * scales[kh] for kh in range(N_KV)], axis=0)
    return jnp.concatenate([ot[:, j * Q_BLOCK:(j + 1) * Q_BLOCK] for j in range(GQA_GROUP)], axis=0)


def _winattn_kernel(sink_ref, x_ref, qt_ref, kp_ref, kc_ref, kn_ref, vp_ref, vc_ref, vn_ref, wo_ref, o_ref,
                    *, blocks):
    i = pl.program_id(1)
    last = pl.num_programs(1) - 1
    qt = qt_ref[0]
    k_all = jnp.concatenate([kp_ref[0], kc_ref[0], kn_ref[0]], axis=0)
    vt_all = jnp.concatenate([vp_ref[0], vc_ref[0], vn_ref[0]], axis=1)
    outs = []
    for blk in range(blocks):
        lo = blk * Q_BLOCK
        prev_valid = (i > 0) if blk == 0 else True
        next_valid = (i < last) if blk == blocks - 1 else True
        outs.append(_attend_block(qt[:, lo:lo + Q_BLOCK], k_all[lo:lo + N_KEYS], vt_all[:, lo:lo + N_KEYS],
                                  sink_ref, prev_valid, next_valid))
    at = jnp.concatenate(outs, axis=1).astype(BF16)
    proj = lax.dot_general(at, wo_ref[...], (((0,), (0,)), ((), ())), preferred_element_type=F32)
    o_ref[0] = x_ref[0] + proj


def _winattn_layer(x, qt, k, vt, sinks, wo):
    B, S, D = x.shape
    nq = qt.shape[1]
    blocks = ATTN_BLOCKS
    rows = blocks * Q_BLOCK
    ns = S // rows
    nb = S // Q_BLOCK
    x_spec = pl.BlockSpec((1, rows, D), lambda b, i: (b, i, 0))
    qt_spec = pl.BlockSpec((1, nq, rows), lambda b, i: (b, 0, i))
    prev = lambda i: jnp.maximum(i * blocks - 1, 0)
    nxt = lambda i: jnp.minimum((i + 1) * blocks, nb - 1)
    kern = functools.partial(_winattn_kernel, blocks=blocks)
    return pl.pallas_call(
        kern,
        grid=(B, ns),
        in_specs=[
            pl.BlockSpec(memory_space=pltpu.SMEM),
            x_spec, qt_spec,
            pl.BlockSpec((1, Q_BLOCK, KV_WIDTH), lambda b, i: (b, prev(i), 0)),
            pl.BlockSpec((1, rows, KV_WIDTH), lambda b, i: (b, i, 0)),
            pl.BlockSpec((1, Q_BLOCK, KV_WIDTH), lambda b, i: (b, nxt(i), 0)),
            pl.BlockSpec((1, KV_WIDTH, Q_BLOCK), lambda b, i: (b, 0, prev(i))),
            pl.BlockSpec((1, KV_WIDTH, rows), lambda b, i: (b, 0, i)),
            pl.BlockSpec((1, KV_WIDTH, Q_BLOCK), lambda b, i: (b, 0, nxt(i))),
            _resident(wo.shape),
        ],
        out_specs=x_spec,
        out_shape=jax.ShapeDtypeStruct((B, S, D), F32),
        compiler_params=_params(),
        name="window_attn",
    )(sinks, x, qt, k, k, k, vt, vt, vt, wo)


def _trunk(x, mem, p):
    B, S, D = x.shape
    N = B * S
    mt_mem, vw_mem = _mem_fold(mem, p["norm_mem"], p["x_wkv"], p["x_wq"], p["x_wo"])
    tables = _rope_tables(S)

    def row(a, i):
        return a[i].reshape(1, -1)

    x = _pool_layer(x, row(p["norm_mix"], 0), p["pool_w"][0], row(p["pool_scale"], 0))
    x2 = x.reshape(N, D)
    for i in range(DEPTH):
        if i % 2 == 1:
            j = i // 2
            qt, k, vt = _qkv_layer(x2, S, row(p["norm_mix"], i), p["attn_wqvt"][j], p["attn_wk"][j], tables)
            x2 = _winattn_layer(x2.reshape(B, S, D), qt, k.reshape(B, S, -1), vt,
                                p["attn_sink"][j], p["attn_o"][j]).reshape(N, D)
        x2 = _xattn_mlp_layer(x2, S, i, row(p["norm_x"], i), mt_mem, vw_mem, row(p["norm_mlp"], i),
                              p["w_up"][i], p["w_down"][i], p["norm_final"].reshape(1, -1),
                              final_norm=(i == DEPTH - 1))
    return x2.reshape(B, S, D)


def kernel(x_prompt, x_sample, mem_prompt, mem_sample, norm_mix, pool_w, pool_scale, attn_qkv, attn_o,
           attn_sink, norm_x, norm_mem, x_wq, x_wkv, x_wo, norm_mlp, w_up, w_down, norm_final):
    nq, nk = N_HEADS * HEAD_DIM, N_KV * HEAD_DIM
    perm = _head_perm()
    p = dict(
        norm_mix=norm_mix, pool_scale=pool_scale, attn_sink=attn_sink, norm_x=norm_x,
        norm_mem=norm_mem.reshape(DEPTH, 1, D_MODEL), norm_mlp=norm_mlp, norm_final=norm_final,
        pool_w=pool_w.astype(BF16),
        attn_wqvt=jnp.swapaxes(
            jnp.concatenate([attn_qkv[:, :, :nq][:, :, perm], attn_qkv[:, :, nq + nk:]], axis=2), 1, 2).astype(BF16),
        attn_wk=attn_qkv[:, :, nq:nq + nk].astype(BF16),
        attn_o=attn_o[:, perm, :].astype(BF16),
        x_wq=x_wq.astype(BF16), x_wkv=x_wkv.astype(BF16), x_wo=x_wo.astype(BF16),
        w_up=w_up.astype(BF16), w_down=w_down.astype(BF16),
    )
    return (_trunk(x_prompt, mem_prompt, p), _trunk(x_sample, mem_sample, p))
```

```python
import functools

import jax
import jax.numpy as jnp
from jax import lax
from jax.experimental import pallas as pl
from jax.experimental.pallas import tpu as pltpu

D_MODEL = 1024
DEPTH = 2
EPS = 1e-6
POOL_WINDOWS = (2, 4, 8, 16)
GROUP = D_MODEL // len(POOL_WINDOWS)
HEAD_DIM = 64
N_HEADS = 16
N_KV = 4
GQA_GROUP = N_HEADS // N_KV
WINDOW = 128
ROPE_THETA = 10000.0
LOG2E = 1.4426950408889634
N_MEM = 256
X_HEADS = 4
X_HEAD_DIM = D_MODEL // X_HEADS
D_FF = 4 * D_MODEL

LANES = 128
SUBLANES = 8
POOL_HALO = 16
POOL_PAD = SUBLANES
VMEM_LIMIT = 60 * 1024 * 1024

TOKEN_TILE = 1024
STREAM_TILE = 2048
FF_CHUNK = 512
Q_BLOCK = 128
ATTN_BLOCKS = 8
N_KEYS = Q_BLOCK + 2 * WINDOW
KV_WIDTH = N_KV * HEAD_DIM

BF16 = jnp.bfloat16
F32 = jnp.float32


def _params():
    return pltpu.CompilerParams(vmem_limit_bytes=VMEM_LIMIT)


def _resident(shape):
    zeros = (0,) * len(shape)
    return pl.BlockSpec(shape, lambda *_: zeros, pipeline_mode=pl.Buffered(1))


def _rms(x, g):
    ms = jnp.mean(x * x, axis=-1, keepdims=True)
    return (x * lax.rsqrt(ms + EPS)) * g


def _dot(a, b):
    return jnp.dot(a, b, preferred_element_type=F32)


def _dot_nt(a, b):
    return lax.dot_general(a, b, (((1,), (1,)), ((), ())), preferred_element_type=F32)


def _pool_kernel(xc_ref, xp_ref, xn_ref, g_ref, w_ref, sc_ref, o_ref, hbuf, s1, s2, *, seq_len, tile):
    i = pl.program_id(1)
    last = pl.num_programs(1) - 1
    g = g_ref[...]
    x = xc_ref[0]
    hn = _rms(x, g)
    hp = _rms(xp_ref[0], g) * jnp.where(i > 0, 1.0, 0.0)
    hx = _rms(xn_ref[0], g) * jnp.where(i < last, 1.0, 0.0)
    H = POOL_HALO
    hbuf[0:H, :] = hp
    hbuf[H:H + tile, :] = hn
    hbuf[H + tile:H + tile + H, :] = hx
    hbuf[H + tile + H:, :] = jnp.zeros((POOL_PAD, hbuf.shape[1]), F32)

    n8 = tile + H
    n4 = n8 + POOL_PAD
    n2 = n4 + POOL_PAD
    t = i * tile + lax.broadcasted_iota(jnp.int32, (tile, 1), 0)
    left = jnp.minimum(t, H).astype(F32)
    right = jnp.minimum(seq_len - t, H).astype(F32)
    for gi, w in enumerate(POOL_WINDOWS):
        cs = slice(gi * GROUP, (gi + 1) * GROUP)
        if w == 2:
            total = hbuf[pl.ds(H - 1, tile), cs] + hbuf[pl.ds(H, tile), cs]
        else:
            s1[0:n2, :] = hbuf[0:n2, cs] + hbuf[1:n2 + 1, cs]
            if w == 4:
                total = s1[pl.ds(H - 2, tile), :] + s1[pl.ds(H, tile), :]
            else:
                s2[0:n4, :] = s1[0:n4, :] + s1[2:n4 + 2, :]
                if w == 8:
                    total = s2[pl.ds(H - 4, tile), :] + s2[pl.ds(H, tile), :]
                else:
                    s1[0:n8, :] = s2[0:n8, :] + s2[4:n8 + 4, :]
                    total = s1[pl.ds(H - 8, tile), :] + s1[pl.ds(H, tile), :]
        count = jnp.minimum(left, float(w // 2)) + jnp.minimum(right, float(w // 2))
        diff = total * (1.0 / count) - hn[:, cs]
        y = _dot(diff.astype(BF16), w_ref[gi])
        o_ref[0, :, cs] = x[:, cs] + y * sc_ref[:, cs]


def _pool_layer(x, g, w, scale):
    B, S, D = x.shape
    tile = STREAM_TILE
    nt = S // tile
    hb = tile // POOL_HALO
    nhb = S // POOL_HALO
    kern = functools.partial(_pool_kernel, seq_len=S, tile=tile)
    return pl.pallas_call(
        kern,
        grid=(B, nt),
        in_specs=[
            pl.BlockSpec((1, tile, D), lambda b, i: (b, i, 0)),
            pl.BlockSpec((1, POOL_HALO, D), lambda b, i: (b, jnp.maximum(i * hb - 1, 0), 0)),
            pl.BlockSpec((1, POOL_HALO, D), lambda b, i: (b, jnp.minimum((i + 1) * hb, nhb - 1), 0)),
            _resident((1, D)),
            _resident(w.shape),
            _resident((1, D)),
        ],
        out_specs=pl.BlockSpec((1, tile, D), lambda b, i: (b, i, 0)),
        out_shape=jax.ShapeDtypeStruct(x.shape, F32),
        scratch_shapes=[pltpu.VMEM((tile + 2 * POOL_HALO + POOL_PAD, D), F32),
                        pltpu.VMEM((tile + 2 * POOL_HALO, GROUP), F32),
                        pltpu.VMEM((tile + 2 * POOL_HALO, GROUP), F32)],
        compiler_params=_params(),
        name="pool_mixer",
    )(x, x, x, g, w, scale)


def _memfold_kernel(m_ref, g_ref, wkv_ref, wq_ref, wo_ref, mt_ref, vw_ref):
    mn = _rms(m_ref[0], g_ref[0]).astype(BF16)
    kv = _dot(mn, wkv_ref[0])
    k = kv[:, :D_MODEL].astype(BF16)
    v = kv[:, D_MODEL:].astype(BF16)
    for h in range(X_HEADS):
        cs = slice(h * X_HEAD_DIM, (h + 1) * X_HEAD_DIM)
        rs = slice(h * N_MEM, (h + 1) * N_MEM)
        mt = _dot_nt(k[:, cs], wq_ref[0, :, cs]) * (X_HEAD_DIM ** -0.5)
        mt_ref[0, 0, rs, :] = mt.astype(BF16)
        vw_ref[0, 0, rs, :] = _dot(v[:, cs], wo_ref[0, cs, :]).astype(BF16)


def _mem_fold(mem, g, w_kv, w_q, w_o):
    B = mem.shape[0]
    rows = X_HEADS * N_MEM
    out = jax.ShapeDtypeStruct((DEPTH, B, rows, D_MODEL), BF16)
    w_spec = pl.BlockSpec((1, D_MODEL, D_MODEL), lambda l, b: (l, 0, 0))
    o_spec = pl.BlockSpec((1, 1, rows, D_MODEL), lambda l, b: (l, b, 0, 0))
    return pl.pallas_call(
        _memfold_kernel,
        grid=(DEPTH, B),
        in_specs=[
            pl.BlockSpec((1, N_MEM, D_MODEL), lambda l, b: (b, 0, 0)),
            pl.BlockSpec((1, 1, D_MODEL), lambda l, b: (l, 0, 0)),
            pl.BlockSpec((1, D_MODEL, 2 * D_MODEL), lambda l, b: (l, 0, 0)),
            w_spec, w_spec,
        ],
        out_specs=[o_spec, o_spec],
        out_shape=[out, out],
        compiler_params=_params(),
        name="mem_fold",
    )(mem, g, w_kv, w_q, w_o)


def _xattn_mlp_kernel(x_ref, gx_ref, mt_ref, vw_ref, gm_ref, wu_ref, wd_ref, gf_ref, o_ref, *, final_norm):
    x = x_ref[...]
    hn = _rms(x, gx_ref[...]).astype(BF16)
    st = _dot_nt(mt_ref[0, 0], hn)
    probs = []
    for h in range(X_HEADS):
        s = st[h * N_MEM:(h + 1) * N_MEM]
        m = jnp.max(s, axis=0, keepdims=True)
        e = jnp.exp(s - m)
        r = 1.0 / jnp.sum(e, axis=0, keepdims=True)
        probs.append((e * r).astype(BF16))
    pt = jnp.concatenate(probs, axis=0)
    x = x + lax.dot_general(pt, vw_ref[0, 0], (((0,), (0,)), ((), ())), preferred_element_type=F32)

    hn = _rms(x, gm_ref[...]).astype(BF16)
    acc = x
    for c in range(D_FF // FF_CHUNK):
        cs = slice(c * FF_CHUNK, (c + 1) * FF_CHUNK)
        u = jnp.maximum(_dot(hn, wu_ref[:, cs]), 0.0)
        acc = acc + _dot((u * u).astype(BF16), wd_ref[cs, :])
    if final_norm:
        acc = _rms(acc, gf_ref[...])
    o_ref[...] = acc


def _xattn_mlp_layer(x2d, seq_len, layer, gx, mt, vw, gm, w_up, w_down, g_final, final_norm):
    N, D = x2d.shape
    tile = TOKEN_TILE
    tps = seq_len // tile
    fold_spec = pl.BlockSpec((1, 1, X_HEADS * N_MEM, D), lambda i: (layer, i // tps, 0, 0))
    kern = functools.partial(_xattn_mlp_kernel, final_norm=final_norm)
    return pl.pallas_call(
        kern,
        grid=(N // tile,),
        in_specs=[
            pl.BlockSpec((tile, D), lambda i: (i, 0)),
            _resident((1, D)),
            fold_spec,
            fold_spec,
            _resident((1, D)),
            _resident((D, D_FF)),
            _resident((D_FF, D)),
            _resident((1, D)),
        ],
        out_specs=pl.BlockSpec((tile, D), lambda i: (i, 0)),
        out_shape=jax.ShapeDtypeStruct((N, D), F32),
        compiler_params=_params(),
        name="xattn_mlp",
    )(x2d, gx, mt, vw, gm, w_up, w_down, g_final)


def _head_perm():
    idx = []
    for j in range(GQA_GROUP):
        for kh in range(N_KV):
            h = GQA_GROUP * kh + j
            idx.extend(range(h * HEAD_DIM, (h + 1) * HEAD_DIM))
    return jnp.asarray(idx, dtype=jnp.int32)


def _rope_tables(seq_len):
    inv_freq = ROPE_THETA ** (-jnp.arange(0, HEAD_DIM, 2, dtype=F32) / HEAD_DIM)
    ang = jnp.arange(seq_len, dtype=F32)[:, None] * inv_freq[None, :]
    cos, sin = jnp.cos(ang), jnp.sin(ang)
    zero = jnp.zeros_like(sin)
    reps = LANES // HEAD_DIM
    cos_t = jnp.tile(jnp.concatenate([cos, cos], axis=-1), (1, reps))
    sa_t = jnp.tile(jnp.concatenate([-sin, zero], axis=-1), (1, reps))
    sb_t = jnp.tile(jnp.concatenate([zero, sin], axis=-1), (1, reps))
    return cos_t, sa_t, sb_t, cos.T, sin.T


def _qkv_kernel(x_ref, g_ref, wqvt_ref, wk_ref, cos_ref, sa_ref, sb_ref, cost_ref, sint_ref,
                qt_ref, k_ref, vt_ref):
    hn = _rms(x_ref[...], g_ref[...]).astype(BF16)
    half = HEAD_DIM // 2
    nq = N_HEADS * HEAD_DIM

    qvt = _dot_nt(wqvt_ref[...], hn)
    cost, sint = cost_ref[...], sint_ref[...]
    scale = HEAD_DIM ** -0.5 * LOG2E
    for h in range(N_HEADS):
        a = qvt[h * HEAD_DIM: h * HEAD_DIM + half]
        b = qvt[h * HEAD_DIM + half: (h + 1) * HEAD_DIM]
        qt_ref[0, h * HEAD_DIM: h * HEAD_DIM + half, :] = ((a * cost - b * sint) * scale).astype(BF16)
        qt_ref[0, h * HEAD_DIM + half: (h + 1) * HEAD_DIM, :] = ((b * cost + a * sint) * scale).astype(BF16)
    vt_ref[0] = qvt[nq:].astype(BF16)

    k = _dot(hn, wk_ref[...])
    cos, sa, sb = cos_ref[...], sa_ref[...], sb_ref[...]
    for cb in range(KV_WIDTH // LANES):
        cs = slice(cb * LANES, (cb + 1) * LANES)
        xb = k[:, cs]
        r = xb * cos + pltpu.roll(xb, LANES - half, 1) * sa + pltpu.roll(xb, half, 1) * sb
        k_ref[:, cs] = r.astype(BF16)


def _qkv_layer(x2d, seq_len, g, wqvt, wk, tables):
    N, D = x2d.shape
    tile = STREAM_TILE
    tps = seq_len // tile
    B = N // seq_len
    nq = N_HEADS * HEAD_DIM
    cos_t, sa_t, sb_t, cost, sint = tables
    tab_spec = pl.BlockSpec((tile, LANES), lambda i: (i % tps, 0))
    tabt_spec = pl.BlockSpec((HEAD_DIM // 2, tile), lambda i: (0, i % tps))
    return pl.pallas_call(
        _qkv_kernel,
        grid=(N // tile,),
        in_specs=[
            pl.BlockSpec((tile, D), lambda i: (i, 0)),
            _resident((1, D)),
            _resident(wqvt.shape),
            _resident(wk.shape),
            tab_spec, tab_spec, tab_spec,
            tabt_spec, tabt_spec,
        ],
        out_specs=[
            pl.BlockSpec((1, nq, tile), lambda i: (i // tps, 0, i % tps)),
            pl.BlockSpec((tile, KV_WIDTH), lambda i: (i, 0)),
            pl.BlockSpec((1, KV_WIDTH, tile), lambda i: (i // tps, 0, i % tps)),
        ],
        out_shape=[
            jax.ShapeDtypeStruct((B, nq, seq_len), BF16),
            jax.ShapeDtypeStruct((N, KV_WIDTH), BF16),
            jax.ShapeDtypeStruct((B, KV_WIDTH, seq_len), BF16),
        ],
        compiler_params=_params(),
        name="qkv_rope",
    )(x2d, g, wqvt, wk, cos_t, sa_t, sb_t, cost, sint)


def _attend_block(qt, k, vt, sink_ref, prev_valid, next_valid):
    slab = KV_WIDTH
    qst = jnp.concatenate([qt[j * slab:(j + 1) * slab] for j in range(GQA_GROUP)], axis=1)
    lane_head = lax.shift_right_logical(
        lax.broadcasted_iota(jnp.int32, k.shape, 1), HEAD_DIM.bit_length() - 1)
    kbd = jnp.concatenate([jnp.where(lane_head == kh, k, jnp.zeros_like(k)) for kh in range(N_KV)], axis=0)
    s = _dot(kbd, qst)

    lanes = GQA_GROUP * Q_BLOCK
    r = lax.broadcasted_iota(jnp.int32, (WINDOW, lanes), 0)
    c = lax.broadcasted_iota(jnp.int32, (WINDOW, lanes), 1) & (Q_BLOCK - 1)
    prev_bias = jnp.where((r >= c) & prev_valid, 0.0, -1e30)
    next_bias = jnp.where((r <= c) & next_valid, 0.0, -1e30)

    probs, scales = [], []
    for kh in range(N_KV):
        base = kh * N_KEYS
        sk = jnp.concatenate([
            s[base:base + WINDOW] + prev_bias,
            s[base + WINDOW:base + WINDOW + Q_BLOCK],
            s[base + WINDOW + Q_BLOCK:base + N_KEYS] + next_bias], axis=0)
        sink = jnp.concatenate(
            [jnp.full((1, Q_BLOCK), sink_ref[GQA_GROUP * kh + j] * LOG2E, F32) for j in range(GQA_GROUP)], axis=1)
        m = jnp.maximum(jnp.max(sk, axis=0, keepdims=True), sink)
        e = jnp.exp2(sk - m)
        denom = jnp.sum(e, axis=0, keepdims=True) + jnp.exp2(sink - m)
        probs.append(e.astype(BF16))
        scales.append(1.0 / denom)
    pt = jnp.concatenate(probs, axis=0)

    zero_v = jnp.zeros((HEAD_DIM, N_KEYS), BF16)
    vbd = jnp.concatenate(
        [jnp.concatenate([vt[kh * HEAD_DIM:(kh + 1) * HEAD_DIM] if rr == kh else zero_v
                          for rr in range(N_KV)], axis=0)
         for kh in range(N_KV)], axis=1)
    ot = _dot(vbd, pt)
    ot = jnp.concatenate([ot[kh * HEAD_DIM:(kh + 1) * HEAD_DIM] * scales[kh] for kh in range(N_KV)], axis=0)
    return jnp.concatenate([ot[:, j * Q_BLOCK:(j + 1) * Q_BLOCK] for j in range(GQA_GROUP)], axis=0)


def _winattn_kernel(sink_ref, x_ref, qt_ref, kp_ref, kc_ref, kn_ref, vp_ref, vc_ref, vn_ref, wo_ref, o_ref,
                    *, blocks):
    i = pl.program_id(1)
    last = pl.num_programs(1) - 1
    qt = qt_ref[0]
    k_all = jnp.concatenate([kp_ref[0], kc_ref[0], kn_ref[0]], axis=0)
    vt_all = jnp.concatenate([vp_ref[0], vc_ref[0], vn_ref[0]], axis=1)
    outs = []
    for blk in range(blocks):
        lo = blk * Q_BLOCK
        prev_valid = (i > 0) if blk == 0 else True
        next_valid = (i < last) if blk == blocks - 1 else True
        outs.append(_attend_block(qt[:, lo:lo + Q_BLOCK], k_all[lo:lo + N_KEYS], vt_all[:, lo:lo + N_KEYS],
                                  sink_ref, prev_valid, next_valid))
    at = jnp.concatenate(outs, axis=1).astype(BF16)
    proj = lax.dot_general(at, wo_ref[...], (((0,), (0,)), ((), ())), preferred_element_type=F32)
    o_ref[0] = x_ref[0] + proj


def _winattn_layer(x, qt, k, vt, sinks, wo):
    B, S, D = x.shape
    nq = qt.shape[1]
    blocks = ATTN_BLOCKS
    rows = blocks * Q_BLOCK
    ns = S // rows
    nb = S // Q_BLOCK
    x_spec = pl.BlockSpec((1, rows, D), lambda b, i: (b, i, 0))
    qt_spec = pl.BlockSpec((1, nq, rows), lambda b, i: (b, 0, i))
    prev = lambda i: jnp.maximum(i * blocks - 1, 0)
    nxt = lambda i: jnp.minimum((i + 1) * blocks, nb - 1)
    kern = functools.partial(_winattn_kernel, blocks=blocks)
    return pl.pallas_call(
        kern,
        grid=(B, ns),
        in_specs=[
            pl.BlockSpec(memory_space=pltpu.SMEM),
            x_spec, qt_spec,
            pl.BlockSpec((1, Q_BLOCK, KV_WIDTH), lambda b, i: (b, prev(i), 0)),
            pl.BlockSpec((1, rows, KV_WIDTH), lambda b, i: (b, i, 0)),
            pl.BlockSpec((1, Q_BLOCK, KV_WIDTH), lambda b, i: (b, nxt(i), 0)),
            pl.BlockSpec((1, KV_WIDTH, Q_BLOCK), lambda b, i: (b, 0, prev(i))),
            pl.BlockSpec((1, KV_WIDTH, rows), lambda b, i: (b, 0, i)),
            pl.BlockSpec((1, KV_WIDTH, Q_BLOCK), lambda b, i: (b, 0, nxt(i))),
            _resident(wo.shape),
        ],
        out_specs=x_spec,
        out_shape=jax.ShapeDtypeStruct((B, S, D), F32),
        compiler_params=_params(),
        name="window_attn",
    )(sinks, x, qt, k, k, k, vt, vt, vt, wo)


def _trunk(x, mem, p):
    B, S, D = x.shape
    N = B * S
    mt_mem, vw_mem = _mem_fold(mem, p["norm_mem"], p["x_wkv"], p["x_wq"], p["x_wo"])
    tables = _rope_tables(S)

    def row(a, i):
        return a[i].reshape(1, -1)

    x = _pool_layer(x, row(p["norm_mix"], 0), p["pool_w"][0], row(p["pool_scale"], 0))
    x2 = x.reshape(N, D)
    for i in range(DEPTH):
        if i % 2 == 1:
            j = i // 2
            qt, k, vt = _qkv_layer(x2, S, row(p["norm_mix"], i), p["attn_wqvt"][j], p["attn_wk"][j], tables)
            x2 = _winattn_layer(x2.reshape(B, S, D), qt, k.reshape(B, S, -1), vt,
                                p["attn_sink"][j], p["attn_o"][j]).reshape(N, D)
        x2 = _xattn_mlp_layer(x2, S, i, row(p["norm_x"], i), mt_mem, vw_mem, row(p["norm_mlp"], i),
                              p["w_up"][i], p["w_down"][i], p["norm_final"].reshape(1, -1),
                              final_norm=(i == DEPTH - 1))
    return x2.reshape(B, S, D)


def kernel(x_prompt, x_sample, mem_prompt, mem_sample, norm_mix, pool_w, pool_scale, attn_qkv, attn_o,
           attn_sink, norm_x, norm_mem, x_wq, x_wkv, x_wo, norm_mlp, w_up, w_down, norm_final):
    nq, nk = N_HEADS * HEAD_DIM, N_KV * HEAD_DIM
    perm = _head_perm()
    p = dict(
        norm_mix=norm_mix, pool_scale=pool_scale, attn_sink=attn_sink, norm_x=norm_x,
        norm_mem=norm_mem.reshape(DEPTH, 1, D_MODEL), norm_mlp=norm_mlp, norm_final=norm_final,
        pool_w=pool_w.astype(BF16),
        attn_wqvt=jnp.swapaxes(
            jnp.concatenate([attn_qkv[:, :, :nq][:, :, perm], attn_qkv[:, :, nq + nk:]], axis=2), 1, 2).astype(BF16),
        attn_wk=attn_qkv[:, :, nq:nq + nk].astype(BF16),
        attn_o=attn_o[:, perm, :].astype(BF16),
        x_wq=x_wq.astype(BF16), x_wkv=x_wkv.astype(BF16), x_wo=x_wo.astype(BF16),
        w_up=w_up.astype(BF16), w_down=w_down.astype(BF16),
    )
    return (_trunk(x_prompt, mem_prompt, p), _trunk(x_sample, mem_sample, p))
```
